```python
import jax, jax.numpy as jnp
from jax import lax
import numpy as np

D_MODEL = 1024
BATCH = 4
SEQ = 8192
DEPTH = 4
DEC_BATCH = 16
DEC_SEQ = 16
PAST_LEN = 4096

CHUNK = 64
N_MIXERS = 2
N_HEADS = 8
HEAD_DIM = 128
ATTN_WIDTH = N_HEADS * HEAD_DIM
CONV_WIDTH = D_MODEL
CONV_K = 3
Q_BLOCK = 128
NORM_EPS = 1e-6
FORGET_BIAS = 3.0

kernel_name = "fox_shortconv_hybrid_stream_step"


def rms_norm(x, g):
    xf = x.astype(jnp.float32)
    y = xf * lax.rsqrt(jnp.mean(xf * xf, axis=-1, keepdims=True) + NORM_EPS)
    return (y * g.astype(jnp.float32)).astype(x.dtype)


def fox_project(h, w_in, b_f, g_q, g_k):
    B, T, _ = h.shape
    z = h @ w_in
    q, k, v, gate = jnp.split(z[..., :4 * ATTN_WIDTH], 4, axis=-1)
    logf = jax.nn.log_sigmoid((z[..., 4 * ATTN_WIDTH:] + b_f).astype(jnp.float32))
    q = rms_norm(q.reshape(B, T, N_HEADS, HEAD_DIM), g_q)
    k = rms_norm(k.reshape(B, T, N_HEADS, HEAD_DIM), g_k)
    v = v.reshape(B, T, N_HEADS, HEAD_DIM)
    return q, k, v, gate, logf


def fox_attend(q, k, v, cq, ck, qpos, kpos):
    s = jnp.einsum("bqhd,bkhd->bhqk", q, k).astype(jnp.float32) * (HEAD_DIM ** -0.5)
    s = s + jnp.transpose(cq, (0, 2, 1))[:, :, :, None] - jnp.transpose(ck, (0, 2, 1))[:, :, None, :]
    mask = kpos[None, :] <= qpos[:, None]
    s = jnp.where(mask[None, None], s, -jnp.inf)
    p = jax.nn.softmax(s, axis=-1).astype(v.dtype)
    return jnp.einsum("bhqk,bkhd->bqhd", p, v)


def fox_out(x, o, gate, w_out):
    B, T = o.shape[0], o.shape[1]
    return x + (jax.nn.silu(gate) * o.reshape(B, T, ATTN_WIDTH)) @ w_out


def fox_prompt(x, g_norm, w_in, b_f, g_q, g_k, w_out):
    B, S, _ = x.shape
    q, k, v, gate, logf = fox_project(rms_norm(x, g_norm), w_in, b_f, g_q, g_k)
    c = jnp.cumsum(logf, axis=1)
    nb = S // Q_BLOCK
    pos = jnp.arange(S)
    qb = jnp.transpose(q.reshape(B, nb, Q_BLOCK, N_HEADS, HEAD_DIM), (1, 0, 2, 3, 4))
    cb = jnp.transpose(c.reshape(B, nb, Q_BLOCK, N_HEADS), (1, 0, 2, 3))
    pb = pos.reshape(nb, Q_BLOCK)
    ob = lax.map(lambda a: fox_attend(a[0], k, v, a[1], c, a[2], pos), (qb, cb, pb))
    o = jnp.transpose(ob, (1, 0, 2, 3, 4)).reshape(B, S, N_HEADS, HEAD_DIM)
    return fox_out(x, o, gate, w_out), k, v, logf


def fox_sample(x, cache_k, cache_v, cache_logf, g_norm, w_in, b_f, g_q, g_k, w_out):
    T = x.shape[1]
    P = cache_k.shape[1]
    q, k, v, gate, logf = fox_project(rms_norm(x, g_norm), w_in, b_f, g_q, g_k)
    k_all = jnp.concatenate([cache_k, k.astype(cache_k.dtype)], axis=1)
    v_all = jnp.concatenate([cache_v, v.astype(cache_v.dtype)], axis=1)
    c = jnp.cumsum(jnp.concatenate([cache_logf.astype(jnp.float32), logf], axis=1), axis=1)
    kpos = jnp.arange(P + T)
    qpos = P + jnp.arange(T)
    o = fox_attend(q, k_all, v_all, c[:, P:], c, qpos, kpos)
    return fox_out(x, o, gate, w_out), k, v, logf


def conv_branch(x, u_hist, g_norm, w_in, conv_w, w_out):
    T = x.shape[1]
    z = rms_norm(x, g_norm) @ w_in
    b, c, u, gate = jnp.split(z, 4, axis=-1)
    u_pad = jnp.concatenate([u_hist.astype(u.dtype), c * u], axis=1)
    conv = conv_w[0] * u_pad[:, 0:T]
    for j in range(1, CONV_K):
        conv = conv + conv_w[j] * u_pad[:, j:j + T]
    y = x + (jax.nn.silu(gate) * b * conv) @ w_out
    return y, u_pad[:, -(CONV_K - 1):]


def setup_inputs(seed: int = 0) -> dict:
    key = jax.random.key(seed)
    ks = iter(jax.random.split(key, 64))

    def nrm(shape, scale=1.0):
        return scale * jax.random.normal(next(ks), shape, jnp.float32)

    d = {}
    d["x_prompt"] = nrm((BATCH, SEQ, D_MODEL))
    d["x_sample"] = nrm((DEC_BATCH, DEC_SEQ, D_MODEL))
    for i in range(DEPTH):
        if i % N_MIXERS == 0:
            d[f"cache_k_l{i}"] = nrm((DEC_BATCH, PAST_LEN, N_HEADS, HEAD_DIM))
            d[f"cache_v_l{i}"] = nrm((DEC_BATCH, PAST_LEN, N_HEADS, HEAD_DIM))
            d[f"cache_logf_l{i}"] = jax.nn.log_sigmoid(FORGET_BIAS + nrm((DEC_BATCH, PAST_LEN, N_HEADS)))
        else:
            d[f"state_conv_l{i}"] = nrm((DEC_BATCH, CONV_K - 1, CONV_WIDTH))
    for i in range(DEPTH):
        d[f"norm_l{i}"] = 1.0 + nrm((D_MODEL,), 0.02)
        if i % N_MIXERS == 0:
            d[f"w_in_l{i}"] = nrm((D_MODEL, 4 * ATTN_WIDTH + N_HEADS), D_MODEL ** -0.5)
            d[f"b_f_l{i}"] = FORGET_BIAS + nrm((N_HEADS,), 0.1)
            d[f"qnorm_l{i}"] = 1.0 + nrm((HEAD_DIM,), 0.02)
            d[f"knorm_l{i}"] = 1.0 + nrm((HEAD_DIM,), 0.02)
            d[f"w_out_l{i}"] = nrm((ATTN_WIDTH, D_MODEL), ATTN_WIDTH ** -0.5)
        else:
            d[f"w_in_l{i}"] = nrm((D_MODEL, 4 * CONV_WIDTH), D_MODEL ** -0.5)
            d[f"conv_w_l{i}"] = nrm((CONV_K, CONV_WIDTH), CONV_K ** -0.5)
            d[f"w_out_l{i}"] = nrm((CONV_WIDTH, D_MODEL), CONV_WIDTH ** -0.5)
    return d


def reference(x_prompt, x_sample,
              cache_k_l0, cache_v_l0, cache_logf_l0, state_conv_l1,
              cache_k_l2, cache_v_l2, cache_logf_l2, state_conv_l3,
              norm_l0, w_in_l0, b_f_l0, qnorm_l0, knorm_l0, w_out_l0,
              norm_l1, w_in_l1, conv_w_l1, w_out_l1,
              norm_l2, w_in_l2, b_f_l2, qnorm_l2, knorm_l2, w_out_l2,
              norm_l3, w_in_l3, conv_w_l3, w_out_l3):
    caches = [(cache_k_l0, cache_v_l0, cache_logf_l0), (state_conv_l1,),
              (cache_k_l2, cache_v_l2, cache_logf_l2), (state_conv_l3,)]
    params = [(norm_l0, w_in_l0, b_f_l0, qnorm_l0, knorm_l0, w_out_l0),
              (norm_l1, w_in_l1, conv_w_l1, w_out_l1),
              (norm_l2, w_in_l2, b_f_l2, qnorm_l2, knorm_l2, w_out_l2),
              (norm_l3, w_in_l3, conv_w_l3, w_out_l3)]
    yp, ys = x_prompt, x_sample
    new = []
    for i in range(DEPTH):
        if i % N_MIXERS == 0:
            yp, kp, vp, lfp = fox_prompt(yp, *params[i])
            ys, ks, vs, lfs = fox_sample(ys, *caches[i], *params[i])
            new.append((kp, vp, lfp, ks, vs, lfs))
        else:
            zero_hist = jnp.zeros((yp.shape[0], CONV_K - 1, CONV_WIDTH), yp.dtype)
            yp, cp = conv_branch(yp, zero_hist, *params[i])
            ys, cs = conv_branch(ys, caches[i][0], *params[i])
            new.append((cp, cs))
    k_p0, v_p0, lf_p0, k_s0, v_s0, lf_s0 = new[0]
    c_p1, c_s1 = new[1]
    k_p2, v_p2, lf_p2, k_s2, v_s2, lf_s2 = new[2]
    c_p3, c_s3 = new[3]
    return (yp, ys,
            k_p0, v_p0, lf_p0, k_s0, v_s0, lf_s0,
            c_p1, c_s1,
            k_p2, v_p2, lf_p2, k_s2, v_s2, lf_s2,
            c_p3, c_s3)
```

```python
import functools

import jax
import jax.numpy as jnp
from jax import lax
from jax.experimental import pallas as pl
from jax.experimental.pallas import tpu as pltpu

N_HEADS = 8
HEAD_DIM = 128
D_MODEL = 1024
CONV_K = 3
NORM_EPS = 1e-6
QK_SCALE = HEAD_DIM ** -0.5
NEG_BIG = -1e30

LANES = 128
V7X_SCOPED_VMEM_BYTES = 60000 * 1024

F32 = jnp.float32
BF16 = jnp.bfloat16


def _params(semantics, vmem_bytes):
    return pltpu.CompilerParams(dimension_semantics=semantics,
                                vmem_limit_bytes=min(int(vmem_bytes), V7X_SCOPED_VMEM_BYTES))


def _rms_scale(x):
    return lax.rsqrt(jnp.mean(x * x, axis=-1, keepdims=True) + NORM_EPS)


def _dot(a, b):
    return jnp.dot(a, b, preferred_element_type=F32)


def _dot_nt(a, b):
    return lax.dot_general(a, b, (((1,), (1,)), ((), ())), preferred_element_type=F32)


def _silu(g):
    return g / (1.0 + jnp.exp(-g))


def _attn_inproj_kernel(x_ref, gn_ref, wq_ref, wk_ref, wv_ref, wg_ref, wf_ref, bf_ref, gq_ref, gk_ref,
                        q_ref, k_ref, kb_ref, v_ref, vb_ref, g_ref, lf_ref):
    x = x_ref[...]
    hb = (x * _rms_scale(x) * gn_ref[...]).astype(BF16)

    zq = _dot(hb, wq_ref[...])
    zk = _dot(hb, wk_ref[...])
    gq = gq_ref[...] * QK_SCALE
    gk = gk_ref[...]
    for h in range(N_HEADS):
        sl = slice(h * HEAD_DIM, (h + 1) * HEAD_DIM)
        qh = zq[:, sl]
        q_ref[:, sl] = (qh * _rms_scale(qh) * gq).astype(BF16)
        kh = zk[:, sl]
        kn = kh * _rms_scale(kh) * gk
        k_ref[:, sl] = kn
        kb_ref[:, sl] = kn.astype(BF16)

    zv = _dot(hb, wv_ref[...])
    v_ref[...] = zv
    vb_ref[...] = zv.astype(BF16)
    g_ref[...] = _dot(hb, wg_ref[...]).astype(BF16)

    zf = _dot(hb, wf_ref[...]) + bf_ref[...]
    lf = jnp.minimum(zf, 0.0) - jnp.log1p(jnp.exp(-jnp.abs(zf)))
    lf_ref[...] = lf[:, :N_HEADS]


def _attn_inproj(x, gn, wq, wk, wv, wg, wf, bf, gq, gk, tm):
    n = x.shape[0]
    assert n % tm == 0
    row = lambda c: pl.BlockSpec((tm, c), lambda i: (i, 0))
    full = lambda a: pl.BlockSpec(a.shape, lambda i: (0, 0))
    w_bytes = 2 * 2 * (4 * D_MODEL * D_MODEL + D_MODEL * LANES)
    io_bytes = 2 * tm * D_MODEL * (4 + 2 + 4 + 2 + 4 + 2 + 2)
    tmp_bytes = 6 * tm * D_MODEL * 4
    return pl.pallas_call(
        _attn_inproj_kernel,
        grid=(n // tm,),
        in_specs=[row(D_MODEL), full(gn), full(wq), full(wk), full(wv), full(wg), full(wf), full(bf),
                  full(gq), full(gk)],
        out_specs=[row(D_MODEL), row(D_MODEL), row(D_MODEL), row(D_MODEL), row(D_MODEL), row(D_MODEL),
                   row(N_HEADS)],
        out_shape=[jax.ShapeDtypeStruct((n, D_MODEL), BF16),
                   jax.ShapeDtypeStruct((n, D_MODEL), F32),
                   jax.ShapeDtypeStruct((n, D_MODEL), BF16),
                   jax.ShapeDtypeStruct((n, D_MODEL), F32),
                   jax.ShapeDtypeStruct((n, D_MODEL), BF16),
                   jax.ShapeDtypeStruct((n, D_MODEL), BF16),
                   jax.ShapeDtypeStruct((n, N_HEADS), F32)],
        compiler_params=_params(("parallel",), w_bytes + io_bytes + tmp_bytes),
        name="attn_inproj",
    )(x, gn, wq, wk, wv, wg, wf, bf, gq, gk)


def _cumsum_kernel(x_ref, u_ref, o_ref):
    rows, length = x_ref.shape
    u = u_ref[...]
    carry = jnp.zeros((rows, 1), F32)
    for j in range(length // LANES):
        sl = slice(j * LANES, (j + 1) * LANES)
        x = x_ref[:, sl]
        hi = x.astype(BF16)
        r1 = x - hi.astype(F32)
        mid = r1.astype(BF16)
        lo = (r1 - mid.astype(F32)).astype(BF16)
        c = (_dot(hi, u) + _dot(mid, u)) + _dot(lo, u) + carry
        o_ref[:, sl] = c
        carry = c[:, LANES - 1:LANES]


def _cumsum_lanes(x):
    rows, length = x.shape
    assert length % LANES == 0 and rows % 8 == 0
    idx = jnp.arange(LANES)
    u = (idx[:, None] <= idx[None, :]).astype(BF16)
    return pl.pallas_call(
        _cumsum_kernel,
        out_shape=jax.ShapeDtypeStruct((rows, length), F32),
        compiler_params=_params((), 8 * rows * length * 4 + (1 << 22)),
        name="cumsum_lanes",
    )(x, u)


def _flash_kernel(q_ref, k_ref, v_ref, c_ref, o_ref, *, blk):
    i = pl.program_id(2)
    q = q_ref[...]

    def step(j, carry, masked):
        m, l, acc = carry
        off = pl.multiple_of(j * blk, blk)
        kb = k_ref[pl.ds(off, blk), :]
        vb = v_ref[pl.ds(off, blk), :]
        s = _dot_nt(q, kb) - c_ref[j]
        if masked:
            r = lax.broadcasted_iota(jnp.int32, (blk, blk), 0)
            c = lax.broadcasted_iota(jnp.int32, (blk, blk), 1)
            s = jnp.where(c <= r, s, NEG_BIG)
        m_new = jnp.maximum(m, jnp.max(s, axis=-1, keepdims=True))
        p = jnp.exp(s - m_new)
        alpha = jnp.exp(m - m_new)
        l = alpha * l + jnp.sum(p, axis=-1, keepdims=True)
        acc = alpha * acc + _dot(p.astype(BF16), vb)
        return m_new, l, acc

    init = (jnp.full((blk, 1), NEG_BIG, F32), jnp.zeros((blk, 1), F32), jnp.zeros((blk, HEAD_DIM), F32))
    carry = lax.fori_loop(0, i, lambda j, c: step(j, c, False), init)
    m, l, acc = step(i, carry, True)
    o_ref[...] = (acc / l).astype(o_ref.dtype)


def _flash_prompt(q, kb, vb, c, batch, seq, blk):
    nq = seq // blk
    qspec = pl.BlockSpec((blk, HEAD_DIM), lambda b, h, i: (b * nq + i, h))
    kvspec = pl.BlockSpec((seq, HEAD_DIM), lambda b, h, i: (b, h))
    cspec = pl.BlockSpec((None, nq, 1, blk), lambda b, h, i: (b * N_HEADS + h, 0, 0, 0))
    vmem = 2 * 2 * seq * HEAD_DIM * 2 + 4 * blk * HEAD_DIM * 2 + 2 * seq * 4 + 8 * blk * blk * 4
    return pl.pallas_call(
        functools.partial(_flash_kernel, blk=blk),
        grid=(batch, N_HEADS, nq),
        in_specs=[qspec, kvspec, kvspec, cspec],
        out_specs=qspec,
        out_shape=jax.ShapeDtypeStruct(q.shape, BF16),
        compiler_params=_params(("parallel", "parallel", "arbitrary"), vmem),
        name="flash_prompt",
    )(q, kb, vb, c)


def _sample_attn_kernel(q_ref, ck_ref, cv_ref, kn_ref, vn_ref, c_ref, o_ref, *, past, t_new):
    q = q_ref[...]
    c = c_ref[...]
    s1 = _dot_nt(q, ck_ref[...].astype(BF16)) - c[:, :past]
    s2 = _dot_nt(q, kn_ref[...]) - c[:, past:past + t_new]
    r = lax.broadcasted_iota(jnp.int32, (t_new, t_new), 0)
    col = lax.broadcasted_iota(jnp.int32, (t_new, t_new), 1)
    s2 = jnp.where(col <= r, s2, NEG_BIG)
    m = jnp.maximum(jnp.max(s1, axis=-1, keepdims=True), jnp.max(s2, axis=-1, keepdims=True))
    p1 = jnp.exp(s1 - m)
    p2 = jnp.exp(s2 - m)
    l = jnp.sum(p1, axis=-1, keepdims=True) + jnp.sum(p2, axis=-1, keepdims=True)
    acc = _dot(p1.astype(BF16), cv_ref[...].astype(BF16)) + _dot(p2.astype(BF16), vn_ref[...])
    o_ref[...] = (acc / l).astype(o_ref.dtype)


def _sample_attn(q, cache_k, cache_v, kb, vb, c, batch, t_new, past):
    cpad = c.shape[-1]
    nspec = pl.BlockSpec((t_new, HEAD_DIM), lambda b, h: (b, h))
    cachespec = pl.BlockSpec((None, past, HEAD_DIM), lambda b, h: (b, 0, h))
    cspec = pl.BlockSpec((None, 1, cpad), lambda b, h: (b * N_HEADS + h, 0, 0))
    vmem = 2 * 2 * past * HEAD_DIM * 4 + 2 * past * HEAD_DIM * 2 + 6 * t_new * past * 4 + (1 << 22)
    return pl.pallas_call(
        functools.partial(_sample_attn_kernel, past=past, t_new=t_new),
        grid=(batch, N_HEADS),
        in_specs=[nspec, cachespec, cachespec, nspec, nspec, cspec],
        out_specs=nspec,
        out_shape=jax.ShapeDtypeStruct(q.shape, BF16),
        compiler_params=_params(("parallel", "parallel"), vmem),
        name="sample_attn",
    )(q, cache_k, cache_v, kb, vb, c)


def _attn_outproj_kernel(x_ref, g_ref, o_ref, w_ref, y_ref):
    gated = _silu(g_ref[...].astype(F32)) * o_ref[...].astype(F32)
    y_ref[...] = x_ref[...] + _dot(gated.astype(BF16), w_ref[...])


def _attn_outproj(x, gate, o, w, tm):
    n = x.shape[0]
    assert n % tm == 0
    row = pl.BlockSpec((tm, D_MODEL), lambda i: (i, 0))
    vmem = 2 * tm * D_MODEL * (4 + 2 + 2 + 4) + 4 * D_MODEL * D_MODEL + 4 * tm * D_MODEL * 4
    return pl.pallas_call(
        _attn_outproj_kernel,
        grid=(n // tm,),
        in_specs=[row, row, row, pl.BlockSpec(w.shape, lambda i: (0, 0))],
        out_specs=row,
        out_shape=jax.ShapeDtypeStruct(x.shape, F32),
        compiler_params=_params(("parallel",), vmem),
        name="attn_outproj",
    )(x, gate, o, w)


def _conv_layer_kernel(x_ref, hist_ref, gn_ref, wb_ref, wc_ref, wu_ref, wg_ref, cw_ref, wo_ref,
                       y_ref, st_ref, tail_ref, *, tm):
    @pl.when(pl.program_id(1) == 0)
    def _():
        tail_ref[...] = jnp.zeros_like(tail_ref)
        tail_ref[8 - (CONV_K - 1):, :] = hist_ref[...]

    x = x_ref[...]
    hb = (x * _rms_scale(x) * gn_ref[...]).astype(BF16)
    cu = _dot(hb, wc_ref[...]) * _dot(hb, wu_ref[...])

    tail = tail_ref[...]
    h_m2 = tail[6:7, :]
    h_m1 = tail[7:8, :]
    r = lax.broadcasted_iota(jnp.int32, cu.shape, 0)
    s1 = jnp.where(r == 0, h_m1, pltpu.roll(cu, 1, 0))
    s2 = jnp.where(r == 0, h_m2, jnp.where(r == 1, h_m1, pltpu.roll(cu, 2, 0)))
    cw = cw_ref[...]
    conv = cw[0:1, :] * s2 + cw[1:2, :] * s1 + cw[2:3, :] * cu

    tail_ref[...] = cu[tm - 8:, :]
    st_ref[...] = cu[tm - (CONV_K - 1):, :]

    mixed = _silu(_dot(hb, wg_ref[...])) * _dot(hb, wb_ref[...]) * conv
    y_ref[...] = x + _dot(mixed.astype(BF16), wo_ref[...])


def _conv_layer(x, hist, gn, wb, wc, wu, wg, cw, wo, batch, seq, tm):
    assert seq % tm == 0 and tm % 8 == 0 and tm >= 8
    nt = seq // tm
    row = pl.BlockSpec((tm, D_MODEL), lambda b, t: (b * nt + t, 0))
    st = pl.BlockSpec((None, CONV_K - 1, D_MODEL), lambda b, t: (b, 0, 0))
    full = lambda a: pl.BlockSpec(a.shape, lambda b, t: (0, 0))
    vmem = 2 * 2 * 5 * D_MODEL * D_MODEL + 4 * tm * D_MODEL * 4 + 8 * tm * D_MODEL * 4 + (1 << 21)
    return pl.pallas_call(
        functools.partial(_conv_layer_kernel, tm=tm),
        grid=(batch, nt),
        in_specs=[row, st, full(gn), full(wb), full(wc), full(wu), full(wg), full(cw), full(wo)],
        out_specs=[row, st],
        out_shape=[jax.ShapeDtypeStruct(x.shape, F32),
                   jax.ShapeDtypeStruct((batch, CONV_K - 1, D_MODEL), F32)],
        scratch_shapes=[pltpu.VMEM((8, D_MODEL), F32)],
        compiler_params=_params(("parallel", "arbitrary"), vmem),
        name="conv_layer",
    )(x, hist, gn, wb, wc, wu, wg, cw, wo)


def _row(v, width=None):
    v = v.astype(F32).reshape(1, -1)
    if width is not None and v.shape[1] < width:
        v = jnp.pad(v, ((0, 0), (0, width - v.shape[1])))
    return v


def _attn_weights(g_norm, w_in, b_f, g_q, g_k, w_out):
    aw = N_HEADS * HEAD_DIM
    wq, wk, wv, wg = (w_in[:, i * aw:(i + 1) * aw].astype(BF16) for i in range(4))
    wf = jnp.pad(w_in[:, 4 * aw:], ((0, 0), (0, LANES - N_HEADS))).astype(BF16)
    return (_row(g_norm), wq, wk, wv, wg, wf, _row(b_f, LANES), _row(g_q), _row(g_k)), w_out.astype(BF16)


def _heads_to_rows(lf, batch, length):
    return jnp.transpose(lf.reshape(batch, length, N_HEADS), (0, 2, 1)).reshape(batch * N_HEADS, length)


def _attn_prompt_layer(x, params, batch, seq, tm, blk):
    inw, wo = _attn_weights(*params)
    q, k, kb, v, vb, gate, lf = _attn_inproj(x, *inw, tm=tm)
    c = _cumsum_lanes(_heads_to_rows(lf, batch, seq))
    c = c.reshape(batch * N_HEADS, seq // blk, 1, blk)
    o = _flash_prompt(q, kb, vb, c, batch, seq, blk)
    y = _attn_outproj(x, gate, o, wo, tm=tm)
    return y, k, v, lf


def _attn_sample_layer(x, cache_k, cache_v, cache_logf, params, batch, t_new):
    past = cache_k.shape[1]
    inw, wo = _attn_weights(*params)
    q, k, kb, v, vb, gate, lf = _attn_inproj(x, *inw, tm=x.shape[0])
    lf_all = jnp.concatenate([cache_logf.astype(F32), lf.reshape(batch, t_new, N_HEADS)], axis=1)
    total = past + t_new
    padded = -(-total // LANES) * LANES
    lf_rows = _heads_to_rows(lf_all.reshape(batch * total, N_HEADS), batch, total)
    lf_rows = jnp.pad(lf_rows, ((0, 0), (0, padded - total)))
    c = _cumsum_lanes(lf_rows).reshape(batch * N_HEADS, 1, padded)
    o = _sample_attn(q, cache_k.reshape(batch, past, -1), cache_v.reshape(batch, past, -1), kb, vb, c,
                     batch, t_new, past)
    y = _attn_outproj(x, gate, o, wo, tm=x.shape[0])
    return y, k, v, lf


def _conv_weights(g_norm, w_in, conv_w, w_out):
    w = D_MODEL
    wb, wc, wu, wg = (w_in[:, i * w:(i + 1) * w].astype(BF16) for i in range(4))
    return _row(g_norm), wb, wc, wu, wg, conv_w.astype(F32), w_out.astype(BF16)


def kernel(x_prompt, x_sample, cache_k_l0, cache_v_l0, cache_logf_l0, state_conv_l1, cache_k_l2, cache_v_l2, cache_logf_l2, state_conv_l3, norm_l0, w_in_l0, b_f_l0, qnorm_l0, knorm_l0, w_out_l0, norm_l1, w_in_l1, conv_w_l1, w_out_l1, norm_l2, w_in_l2, b_f_l2, qnorm_l2, knorm_l2, w_out_l2, norm_l3, w_in_l3, conv_w_l3, w_out_l3):
    batch, seq, d = x_prompt.shape
    dec_batch, dec_seq, _ = x_sample.shape
    caches = [(cache_k_l0, cache_v_l0, cache_logf_l0), (state_conv_l1,),
              (cache_k_l2, cache_v_l2, cache_logf_l2), (state_conv_l3,)]
    params = [(norm_l0, w_in_l0, b_f_l0, qnorm_l0, knorm_l0, w_out_l0),
              (norm_l1, w_in_l1, conv_w_l1, w_out_l1),
              (norm_l2, w_in_l2, b_f_l2, qnorm_l2, knorm_l2, w_out_l2),
              (norm_l3, w_in_l3, conv_w_l3, w_out_l3)]
    tm, blk = 512, 512
    yp = x_prompt.reshape(batch * seq, d)
    ys = x_sample.reshape(dec_batch * dec_seq, d)
    outs = []
    for i in range(4):
        if i % 2 == 0:
            yp, kp, vp, lfp = _attn_prompt_layer(yp, params[i], batch, seq, tm, blk)
            ys, ks, vs, lfs = _attn_sample_layer(ys, *caches[i], params[i], dec_batch, dec_seq)
            outs += [kp.reshape(batch, seq, N_HEADS, HEAD_DIM), vp.reshape(batch, seq, N_HEADS, HEAD_DIM),
                     lfp.reshape(batch, seq, N_HEADS),
                     ks.reshape(dec_batch, dec_seq, N_HEADS, HEAD_DIM),
                     vs.reshape(dec_batch, dec_seq, N_HEADS, HEAD_DIM),
                     lfs.reshape(dec_batch, dec_seq, N_HEADS)]
        else:
            cw = _conv_weights(*params[i])
            zero_hist = jnp.zeros((batch, CONV_K - 1, d), F32)
            yp, cp = _conv_layer(yp, zero_hist, *cw, batch=batch, seq=seq, tm=tm)
            ys, cs = _conv_layer(ys, caches[i][0].astype(F32), *cw, batch=dec_batch, seq=dec_seq, tm=dec_seq)
            outs += [cp, cs]
    return (yp.reshape(batch, seq, d), ys.reshape(dec_batch, dec_seq, d), *outs)
```

```python
import functools

import jax
import jax.numpy as jnp
from jax import lax
from jax.experimental import pallas as pl
from jax.experimental.pallas import tpu as pltpu

N_HEADS = 8
HEAD_DIM = 128
D_MODEL = 1024
CONV_K = 3
NORM_EPS = 1e-6
LOG2E = 1.4426950408889634
QK_SCALE_LOG2 = HEAD_DIM ** -0.5 * LOG2E
NEG_BIG = -1e30
FAST_LOGIT_BOUND = 60.0

LANES = 128
V7X_SCOPED_VMEM_BYTES = 60000 * 1024

F32 = jnp.float32
BF16 = jnp.bfloat16


def _params(semantics, vmem_bytes):
    return pltpu.CompilerParams(dimension_semantics=semantics,
                                vmem_limit_bytes=min(int(vmem_bytes), V7X_SCOPED_VMEM_BYTES))


def _rms_scale(x):
    return lax.rsqrt(jnp.mean(x * x, axis=-1, keepdims=True) + NORM_EPS)


def _dot(a, b):
    return jnp.dot(a, b, preferred_element_type=F32)


def _dot_nt(a, b):
    return lax.dot_general(a, b, (((1,), (1,)), ((), ())), preferred_element_type=F32)


def _silu(g):
    return g / (1.0 + jnp.exp(-g))


def _attn_inproj_kernel(x_ref, gn_ref, wq_ref, wk_ref, wv_ref, wg_ref, wf_ref, bf_ref, gq_ref, gk_ref,
                        q_ref, k_ref, kb_ref, v_ref, vb_ref, g_ref, lf_ref):
    x = x_ref[...]
    hb = (x * _rms_scale(x) * gn_ref[...]).astype(BF16)

    tm = x.shape[0]
    zq = _dot(hb, wq_ref[...])
    zk = _dot(hb, wk_ref[...])
    zv = _dot(hb, wv_ref[...])
    gq = gq_ref[...] * QK_SCALE_LOG2
    gk = gk_ref[...]
    for h in range(N_HEADS):
        sl = slice(h * HEAD_DIM, (h + 1) * HEAD_DIM)
        rows = pl.ds(h, tm, stride=N_HEADS)
        qh = zq[:, sl]
        q_ref[:, sl] = (qh * _rms_scale(qh) * gq).astype(BF16)
        kh = zk[:, sl]
        kn = kh * _rms_scale(kh) * gk
        k_ref[rows, :] = kn
        kb_ref[:, sl] = kn.astype(BF16)
        v_ref[rows, :] = zv[:, sl]

    vb_ref[...] = zv.astype(BF16)
    g_ref[...] = _dot(hb, wg_ref[...]).astype(BF16)

    zf = _dot(hb, wf_ref[...]) + bf_ref[...]
    lf = jnp.minimum(zf, 0.0) - jnp.log1p(jnp.exp(-jnp.abs(zf)))
    lf_ref[...] = lf[:, :N_HEADS]


def _attn_inproj(x, gn, wq, wk, wv, wg, wf, bf, gq, gk, tm):
    n = x.shape[0]
    assert n % tm == 0
    row = lambda c: pl.BlockSpec((tm, c), lambda i: (i, 0))
    head_rows = pl.BlockSpec((tm * N_HEADS, HEAD_DIM), lambda i: (i, 0))
    full = lambda a: pl.BlockSpec(a.shape, lambda i: (0, 0))
    w_bytes = 2 * 2 * (4 * D_MODEL * D_MODEL + D_MODEL * LANES)
    io_bytes = 2 * tm * D_MODEL * (4 + 2 + 4 + 2 + 4 + 2 + 2)
    tmp_bytes = 6 * tm * D_MODEL * 4
    return pl.pallas_call(
        _attn_inproj_kernel,
        grid=(n // tm,),
        in_specs=[row(D_MODEL), full(gn), full(wq), full(wk), full(wv), full(wg), full(wf), full(bf),
                  full(gq), full(gk)],
        out_specs=[row(D_MODEL), head_rows, row(D_MODEL), head_rows, row(D_MODEL), row(D_MODEL),
                   row(N_HEADS)],
        out_shape=[jax.ShapeDtypeStruct((n, D_MODEL), BF16),
                   jax.ShapeDtypeStruct((n * N_HEADS, HEAD_DIM), F32),
                   jax.ShapeDtypeStruct((n, D_MODEL), BF16),
                   jax.ShapeDtypeStruct((n * N_HEADS, HEAD_DIM), F32),
                   jax.ShapeDtypeStruct((n, D_MODEL), BF16),
                   jax.ShapeDtypeStruct((n, D_MODEL), BF16),
                   jax.ShapeDtypeStruct((n, N_HEADS), F32)],
        compiler_params=_params(("parallel",), w_bytes + io_bytes + tmp_bytes),
        name="attn_inproj",
    )(x, gn, wq, wk, wv, wg, wf, bf, gq, gk)


def _cumsum_kernel(x_ref, u_ref, o_ref):
    rows, length = x_ref.shape
    u = u_ref[...]
    carry = jnp.zeros((rows, 1), F32)
    for j in range(length // LANES):
        sl = slice(j * LANES, (j + 1) * LANES)
        x = x_ref[:, sl]
        hi = x.astype(BF16)
        r1 = x - hi.astype(F32)
        mid = r1.astype(BF16)
        lo = (r1 - mid.astype(F32)).astype(BF16)
        c = (_dot(hi, u) + _dot(mid, u)) + _dot(lo, u) + carry
        o_ref[:, sl] = c * LOG2E
        carry = c[:, LANES - 1:LANES]


def _cumsum_lanes(x):
    rows, length = x.shape
    assert length % LANES == 0 and rows % 8 == 0
    idx = jnp.arange(LANES)
    u = (idx[:, None] <= idx[None, :]).astype(BF16)
    return pl.pallas_call(
        _cumsum_kernel,
        out_shape=jax.ShapeDtypeStruct((rows, length), F32),
        compiler_params=_params((), 8 * rows * length * 4 + (1 << 22)),
        name="cumsum_lanes",
    )(x, u)


def _causal_mask(blk):
    r = lax.broadcasted_iota(jnp.int32, (blk, blk), 0)
    c = lax.broadcasted_iota(jnp.int32, (blk, blk), 1)
    return c <= r


def _flash_online_kernel(q_ref, k_ref, v_ref, c_ref, o_ref, *, blk):
    i = pl.program_id(2)
    q = q_ref[...]

    def step(j, carry, masked):
        m, l, acc = carry
        off = pl.multiple_of(j * blk, blk)
        kb = k_ref[pl.ds(off, blk), :]
        vb = v_ref[pl.ds(off, blk), :]
        s = _dot_nt(q, kb) - c_ref[j]
        if masked:
            s = jnp.where(_causal_mask(blk), s, NEG_BIG)
        m_new = jnp.maximum(m, jnp.max(s, axis=-1, keepdims=True))
        p = jnp.exp2(s - m_new)
        alpha = jnp.exp2(m - m_new)
        l = alpha * l + jnp.sum(p, axis=-1, keepdims=True)
        acc = alpha * acc + _dot(p.astype(BF16), vb)
        return m_new, l, acc

    init = (jnp.full((blk, 1), NEG_BIG, F32), jnp.zeros((blk, 1), F32), jnp.zeros((blk, HEAD_DIM), F32))
    carry = lax.fori_loop(0, i, lambda j, c: step(j, c, False), init)
    m, l, acc = step(i, carry, True)
    o_ref[...] = (acc / l).astype(o_ref.dtype)


def _flash_bounded_kernel(q_ref, k_ref, v_ref, c_ref, o_ref, vones_ref, qbias_ref, p_ref, acc_ref, *, tk):
    i = pl.program_id(2)

    @pl.when(i == 0)
    def _():
        lane = lax.broadcasted_iota(jnp.int32, (v_ref.shape[0], LANES), 1)
        vones_ref[:, :HEAD_DIM] = v_ref[...]
        vones_ref[:, HEAD_DIM:] = jnp.where(lane == 0, 1.0, 0.0).astype(BF16)

    causal = _causal_mask(tk)
    c_top = jnp.transpose(jnp.broadcast_to(c_ref[2 * i], (tk, tk)))
    c_bot = jnp.transpose(jnp.broadcast_to(c_ref[2 * i + 1], (tk, tk)))
    qbias_ref[0, :tk, :] = c_top
    qbias_ref[0, tk:, :] = c_bot
    qbias_ref[1, :tk, :] = jnp.where(causal, c_top, NEG_BIG)
    qbias_ref[1, tk:, :] = c_bot
    qbias_ref[2, :tk, :] = jnp.full((tk, tk), NEG_BIG, F32)
    qbias_ref[2, tk:, :] = jnp.where(causal, c_bot, NEG_BIG)

    def keys(ref, j):
        return ref[pl.ds(pl.multiple_of(j * tk, tk), tk), :]

    def probs(j, bias):
        z = (_dot_nt(q_ref[...], keys(k_ref, j)) + bias) - c_ref[j]
        return jnp.exp2(z).astype(BF16)

    def accumulate(slot, j):
        acc_ref[...] += _dot(p_ref[slot], keys(vones_ref, j))

    first = 2 * i
    p_ref[0] = probs(0, qbias_ref[(i == 0).astype(jnp.int32)])
    acc_ref[...] = jnp.zeros_like(acc_ref)

    def trip(t, _):
        accumulate(0, 2 * t)
        p_ref[1] = probs(2 * t + 1, qbias_ref[0])
        accumulate(1, 2 * t + 1)
        p_ref[0] = probs(2 * t + 2, qbias_ref[(2 * t + 2 == first).astype(jnp.int32)])
        return 0

    lax.fori_loop(0, i, trip, 0)
    accumulate(0, first)
    z = (_dot_nt(q_ref[tk:, :], keys(k_ref, first + 1)) + qbias_ref[2, tk:, :]) - c_ref[first + 1]
    acc_ref[tk:, :] += _dot(jnp.exp2(z).astype(BF16), keys(vones_ref, first + 1))
    acc = acc_ref[...]
    o_ref[...] = (acc[:, :HEAD_DIM] / acc[:, HEAD_DIM:HEAD_DIM + 1]).astype(o_ref.dtype)


def _flash_prompt(q, kb, vb, c, batch, seq, tk, bounded):
    tq = 2 * tk if bounded else tk
    nq = seq // tq
    assert seq % tq == 0
    qspec = pl.BlockSpec((tq, HEAD_DIM), lambda b, h, i: (b * nq + i, h))
    kvspec = pl.BlockSpec((seq, HEAD_DIM), lambda b, h, i: (b, h))
    cspec = pl.BlockSpec((None, seq // tk, 1, tk), lambda b, h, i: (b * N_HEADS + h, 0, 0, 0))
    vmem = 2 * 2 * seq * HEAD_DIM * 2 + 4 * tq * HEAD_DIM * 2 + 2 * seq * 4 + 8 * tq * tk * 4
    if bounded:
        body = functools.partial(_flash_bounded_kernel, tk=tk)
        scratch = [pltpu.VMEM((seq, 2 * HEAD_DIM), BF16),
                   pltpu.VMEM((3, tq, tk), F32),
                   pltpu.VMEM((2, tq, tk), BF16),
                   pltpu.VMEM((tq, 2 * HEAD_DIM), F32)]
        vmem += seq * 2 * HEAD_DIM * 2 + 3 * tq * tk * 4 + 2 * tq * tk * 2 + tq * 2 * HEAD_DIM * 4
    else:
        body = functools.partial(_flash_online_kernel, blk=tk)
        scratch = []
    return pl.pallas_call(
        body,
        grid=(batch, N_HEADS, nq),
        in_specs=[qspec, kvspec, kvspec, cspec],
        out_specs=qspec,
        out_shape=jax.ShapeDtypeStruct(q.shape, BF16),
        scratch_shapes=scratch,
        compiler_params=_params(("parallel", "parallel", "arbitrary"), vmem),
        name="flash_bounded" if bounded else "flash_online",
    )(q, kb, vb, c)


def _sample_attn_kernel(q_ref, ck_ref, cv_ref, kn_ref, vn_ref, c_ref, o_ref, *, past, t_new):
    q = q_ref[...]
    c = c_ref[...]
    s1 = _dot_nt(q, ck_ref[...].astype(BF16)) - c[:, :past]
    s2 = _dot_nt(q, kn_ref[...]) - c[:, past:past + t_new]
    r = lax.broadcasted_iota(jnp.int32, (t_new, t_new), 0)
    col = lax.broadcasted_iota(jnp.int32, (t_new, t_new), 1)
    s2 = jnp.where(col <= r, s2, NEG_BIG)
    m = jnp.maximum(jnp.max(s1, axis=-1, keepdims=True), jnp.max(s2, axis=-1, keepdims=True))
    p1 = jnp.exp2(s1 - m)
    p2 = jnp.exp2(s2 - m)
    l = jnp.sum(p1, axis=-1, keepdims=True) + jnp.sum(p2, axis=-1, keepdims=True)
    acc = _dot(p1.astype(BF16), cv_ref[...].astype(BF16)) + _dot(p2.astype(BF16), vn_ref[...])
    o_ref[...] = (acc / l).astype(o_ref.dtype)


def _sample_attn(q, cache_k, cache_v, kb, vb, c, batch, t_new, past):
    cpad = c.shape[-1]
    nspec = pl.BlockSpec((t_new, HEAD_DIM), lambda b, h: (b, h))
    cachespec = pl.BlockSpec((None, past, HEAD_DIM), lambda b, h: (b, 0, h))
    cspec = pl.BlockSpec((None, 1, cpad), lambda b, h: (b * N_HEADS + h, 0, 0))
    vmem = 2 * 2 * past * HEAD_DIM * 4 + 2 * past * HEAD_DIM * 2 + 6 * t_new * past * 4 + (1 << 22)
    return pl.pallas_call(
        functools.partial(_sample_attn_kernel, past=past, t_new=t_new),
        grid=(batch, N_HEADS),
        in_specs=[nspec, cachespec, cachespec, nspec, nspec, cspec],
        out_specs=nspec,
        out_shape=jax.ShapeDtypeStruct(q.shape, BF16),
        compiler_params=_params(("parallel", "parallel"), vmem),
        name="sample_attn",
    )(q, cache_k, cache_v, kb, vb, c)


def _attn_outproj_kernel(x_ref, g_ref, o_ref, w_ref, y_ref):
    gated = _silu(g_ref[...].astype(F32)) * o_ref[...].astype(F32)
    y_ref[...] = x_ref[...] + _dot(gated.astype(BF16), w_ref[...])


def _attn_outproj(x, gate, o, w, tm):
    n = x.shape[0]
    assert n % tm == 0
    row = pl.BlockSpec((tm, D_MODEL), lambda i: (i, 0))
    vmem = 2 * tm * D_MODEL * (4 + 2 + 2 + 4) + 4 * D_MODEL * D_MODEL + 4 * tm * D_MODEL * 4
    return pl.pallas_call(
        _attn_outproj_kernel,
        grid=(n // tm,),
        in_specs=[row, row, row, pl.BlockSpec(w.shape, lambda i: (0, 0))],
        out_specs=row,
        out_shape=jax.ShapeDtypeStruct(x.shape, F32),
        compiler_params=_params(("parallel",), vmem),
        name="attn_outproj",
    )(x, gate, o, w)


def _conv_layer_kernel(x_ref, hist_ref, gn_ref, wb_ref, wc_ref, wu_ref, wg_ref, cw_ref, wo_ref,
                       y_ref, st_ref, tail_ref, *, tm):
    @pl.when(pl.program_id(1) == 0)
    def _():
        tail_ref[...] = jnp.zeros_like(tail_ref)
        tail_ref[8 - (CONV_K - 1):, :] = hist_ref[...]

    x = x_ref[...]
    hb = (x * _rms_scale(x) * gn_ref[...]).astype(BF16)
    cu = _dot(hb, wc_ref[...]) * _dot(hb, wu_ref[...])

    tail = tail_ref[...]
    h_m2 = tail[6:7, :]
    h_m1 = tail[7:8, :]
    r = lax.broadcasted_iota(jnp.int32, cu.shape, 0)
    s1 = jnp.where(r == 0, h_m1, pltpu.roll(cu, 1, 0))
    s2 = jnp.where(r == 0, h_m2, jnp.where(r == 1, h_m1, pltpu.roll(cu, 2, 0)))
    cw = cw_ref[...]
    conv = cw[0:1, :] * s2 + cw[1:2, :] * s1 + cw[2:3, :] * cu

    tail_ref[...] = cu[tm - 8:, :]
    st_ref[...] = cu[tm - (CONV_K - 1):, :]

    mixed = _silu(_dot(hb, wg_ref[...])) * _dot(hb, wb_ref[...]) * conv
    y_ref[...] = x + _dot(mixed.astype(BF16), wo_ref[...])


def _conv_layer(x, hist, gn, wb, wc, wu, wg, cw, wo, batch, seq, tm):
    assert seq % tm == 0 and tm % 8 == 0 and tm >= 8
    nt = seq // tm
    row = pl.BlockSpec((tm, D_MODEL), lambda b, t: (b * nt + t, 0))
    st = pl.BlockSpec((None, CONV_K - 1, D_MODEL), lambda b, t: (b, 0, 0))
    full = lambda a: pl.BlockSpec(a.shape, lambda b, t: (0, 0))
    vmem = 2 * 2 * 5 * D_MODEL * D_MODEL + 4 * tm * D_MODEL * 4 + 8 * tm * D_MODEL * 4 + (1 << 21)
    return pl.pallas_call(
        functools.partial(_conv_layer_kernel, tm=tm),
        grid=(batch, nt),
        in_specs=[row, st, full(gn), full(wb), full(wc), full(wu), full(wg), full(cw), full(wo)],
        out_specs=[row, st],
        out_shape=[jax.ShapeDtypeStruct(x.shape, F32),
                   jax.ShapeDtypeStruct((batch, CONV_K - 1, D_MODEL), F32)],
        scratch_shapes=[pltpu.VMEM((8, D_MODEL), F32)],
        compiler_params=_params(("parallel", "arbitrary"), vmem),
        name="conv_layer",
    )(x, hist, gn, wb, wc, wu, wg, cw, wo)


def _row(v, width=None):
    v = v.astype(F32).reshape(1, -1)
    if width is not None and v.shape[1] < width:
        v = jnp.pad(v, ((0, 0), (0, width - v.shape[1])))
    return v


def _attn_weights(g_norm, w_in, b_f, g_q, g_k, w_out):
    aw = N_HEADS * HEAD_DIM
    wq, wk, wv, wg = (w_in[:, i * aw:(i + 1) * aw].astype(BF16) for i in range(4))
    wf = jnp.pad(w_in[:, 4 * aw:], ((0, 0), (0, LANES - N_HEADS))).astype(BF16)
    return (_row(g_norm), wq, wk, wv, wg, wf, _row(b_f, LANES), _row(g_q), _row(g_k)), w_out.astype(BF16)


def _heads_to_rows(lf, batch, length):
    return jnp.transpose(lf.reshape(batch, length, N_HEADS), (0, 2, 1)).reshape(batch * N_HEADS, length)


def _attn_prompt_layer(x, params, batch, seq, tm, tk):
    inw, wo = _attn_weights(*params)
    q, k, kb, v, vb, gate, lf = _attn_inproj(x, *inw, tm=tm)
    c = _cumsum_lanes(_heads_to_rows(lf, batch, seq))
    c = c.reshape(batch * N_HEADS, seq // tk, 1, tk)
    g_q, g_k = params[3], params[4]
    logit_bound = 1.01 * HEAD_DIM ** 0.5 * jnp.max(jnp.abs(g_q)) * jnp.max(jnp.abs(g_k))
    o = lax.cond(logit_bound < FAST_LOGIT_BOUND,
                 functools.partial(_flash_prompt, batch=batch, seq=seq, tk=tk, bounded=True),
                 functools.partial(_flash_prompt, batch=batch, seq=seq, tk=tk, bounded=False),
                 q, kb, vb, c)
    y = _attn_outproj(x, gate, o, wo, tm=tm)
    return y, k, v, lf


def _attn_sample_layer(x, cache_k, cache_v, cache_logf, params, batch, t_new):
    past = cache_k.shape[1]
    inw, wo = _attn_weights(*params)
    q, k, kb, v, vb, gate, lf = _attn_inproj(x, *inw, tm=x.shape[0])
    lf_all = jnp.concatenate([cache_logf.astype(F32), lf.reshape(batch, t_new, N_HEADS)], axis=1)
    total = past + t_new
    padded = -(-total // LANES) * LANES
    lf_rows = _heads_to_rows(lf_all.reshape(batch * total, N_HEADS), batch, total)
    lf_rows = jnp.pad(lf_rows, ((0, 0), (0, padded - total)))
    c = _cumsum_lanes(lf_rows).reshape(batch * N_HEADS, 1, padded)
    o = _sample_attn(q, cache_k.reshape(batch, past, -1), cache_v.reshape(batch, past, -1), kb, vb, c,
                     batch, t_new, past)
    y = _attn_outproj(x, gate, o, wo, tm=x.shape[0])
    return y, k, v, lf


def _conv_weights(g_norm, w_in, conv_w, w_out):
    w = D_MODEL
    wb, wc, wu, wg = (w_in[:, i * w:(i + 1) * w].astype(BF16) for i in range(4))
    return _row(g_norm), wb, wc, wu, wg, conv_w.astype(F32), w_out.astype(BF16)


def kernel(x_prompt, x_sample, cache_k_l0, cache_v_l0, cache_logf_l0, state_conv_l1, cache_k_l2, cache_v_l2, cache_logf_l2, state_conv_l3, norm_l0, w_in_l0, b_f_l0, qnorm_l0, knorm_l0, w_out_l0, norm_l1, w_in_l1, conv_w_l1, w_out_l1, norm_l2, w_in_l2, b_f_l2, qnorm_l2, knorm_l2, w_out_l2, norm_l3, w_in_l3, conv_w_l3, w_out_l3):
    batch, seq, d = x_prompt.shape
    dec_batch, dec_seq, _ = x_sample.shape
    caches = [(cache_k_l0, cache_v_l0, cache_logf_l0), (state_conv_l1,),
              (cache_k_l2, cache_v_l2, cache_logf_l2), (state_conv_l3,)]
    params = [(norm_l0, w_in_l0, b_f_l0, qnorm_l0, knorm_l0, w_out_l0),
              (norm_l1, w_in_l1, conv_w_l1, w_out_l1),
              (norm_l2, w_in_l2, b_f_l2, qnorm_l2, knorm_l2, w_out_l2),
              (norm_l3, w_in_l3, conv_w_l3, w_out_l3)]
    tm, tk = 512, 512
    yp = x_prompt.reshape(batch * seq, d)
    ys = x_sample.reshape(dec_batch * dec_seq, d)
    outs = []
    for i in range(4):
        if i % 2 == 0:
            yp, kp, vp, lfp = _attn_prompt_layer(yp, params[i], batch, seq, tm, tk)
            ys, ks, vs, lfs = _attn_sample_layer(ys, *caches[i], params[i], dec_batch, dec_seq)
            outs += [kp.reshape(batch, seq, N_HEADS, HEAD_DIM), vp.reshape(batch, seq, N_HEADS, HEAD_DIM),
                     lfp.reshape(batch, seq, N_HEADS),
                     ks.reshape(dec_batch, dec_seq, N_HEADS, HEAD_DIM),
                     vs.reshape(dec_batch, dec_seq, N_HEADS, HEAD_DIM),
                     lfs.reshape(dec_batch, dec_seq, N_HEADS)]
        else:
            cw = _conv_weights(*params[i])
            zero_hist = jnp.zeros((batch, CONV_K - 1, d), F32)
            yp, cp = _conv_layer(yp, zero_hist, *cw, batch=batch, seq=seq, tm=tm)
            ys, cs = _conv_layer(ys, caches[i][0].astype(F32), *cw, batch=dec_batch, seq=dec_seq, tm=dec_seq)
            outs += [cp, cs]
    return (yp.reshape(batch, seq, d), ys.reshape(dec_batch, dec_seq, d), *outs)
```

```python
import functools

import jax
import jax.numpy as jnp
from jax import lax
from jax.experimental import pallas as pl
from jax.experimental.pallas import tpu as pltpu

N_HEADS = 8
HEAD_DIM = 128
D_MODEL = 1024
CONV_K = 3
NORM_EPS = 1e-6
LOG2E = 1.4426950408889634
QK_SCALE_LOG2 = HEAD_DIM ** -0.5 * LOG2E
NEG_BIG = -1e30
FAST_LOGIT_BOUND = 60.0

LANES = 128
V7X_SCOPED_VMEM_BYTES = 60000 * 1024

F32 = jnp.float32
BF16 = jnp.bfloat16


def _params(semantics, vmem_bytes):
    return pltpu.CompilerParams(dimension_semantics=semantics,
                                vmem_limit_bytes=min(int(vmem_bytes), V7X_SCOPED_VMEM_BYTES))


def _rms_scale(x):
    return lax.rsqrt(jnp.mean(x * x, axis=-1, keepdims=True) + NORM_EPS)


def _dot(a, b):
    return jnp.dot(a, b, preferred_element_type=F32)


def _dot_nt(a, b):
    return lax.dot_general(a, b, (((1,), (1,)), ((), ())), preferred_element_type=F32)


def _silu(g):
    return g / (1.0 + jnp.exp(-g))


def _attn_inproj_kernel(x_ref, gn_ref, wq_ref, wk_ref, wv_ref, wg_ref, wf_ref, bf_ref, gq_ref, gk_ref,
                        q_ref, k_ref, kb_ref, v_ref, vb_ref, g_ref, lf_ref):
    x = x_ref[...]
    hb = (x * _rms_scale(x) * gn_ref[...]).astype(BF16)

    tm = x.shape[0]
    zq = _dot(hb, wq_ref[...])
    zk = _dot(hb, wk_ref[...])
    zv = _dot(hb, wv_ref[...])
    gq = gq_ref[...] * QK_SCALE_LOG2
    gk = gk_ref[...]
    for h in range(N_HEADS):
        sl = slice(h * HEAD_DIM, (h + 1) * HEAD_DIM)
        rows = pl.ds(h, tm, stride=N_HEADS)
        qh = zq[:, sl]
        q_ref[:, sl] = (qh * _rms_scale(qh) * gq).astype(BF16)
        kh = zk[:, sl]
        kn = kh * _rms_scale(kh) * gk
        k_ref[rows, :] = kn
        kb_ref[:, sl] = kn.astype(BF16)
        v_ref[rows, :] = zv[:, sl]

    vb_ref[...] = zv.astype(BF16)
    g_ref[...] = _dot(hb, wg_ref[...]).astype(BF16)

    zf = _dot(hb, wf_ref[...]) + bf_ref[...]
    lf = jnp.minimum(zf, 0.0) - jnp.log1p(jnp.exp(-jnp.abs(zf)))
    lf_ref[...] = lf[:, :N_HEADS]


def _attn_inproj(x, gn, wq, wk, wv, wg, wf, bf, gq, gk, tm):
    n = x.shape[0]
    assert n % tm == 0
    row = lambda c: pl.BlockSpec((tm, c), lambda i: (i, 0))
    head_rows = pl.BlockSpec((tm * N_HEADS, HEAD_DIM), lambda i: (i, 0))
    full = lambda a: pl.BlockSpec(a.shape, lambda i: (0, 0))
    w_bytes = 2 * 2 * (4 * D_MODEL * D_MODEL + D_MODEL * LANES)
    io_bytes = 2 * tm * D_MODEL * (4 + 2 + 4 + 2 + 4 + 2 + 2)
    tmp_bytes = 6 * tm * D_MODEL * 4
    return pl.pallas_call(
        _attn_inproj_kernel,
        grid=(n // tm,),
        in_specs=[row(D_MODEL), full(gn), full(wq), full(wk), full(wv), full(wg), full(wf), full(bf),
                  full(gq), full(gk)],
        out_specs=[row(D_MODEL), head_rows, row(D_MODEL), head_rows, row(D_MODEL), row(D_MODEL),
                   row(N_HEADS)],
        out_shape=[jax.ShapeDtypeStruct((n, D_MODEL), BF16),
                   jax.ShapeDtypeStruct((n * N_HEADS, HEAD_DIM), F32),
                   jax.ShapeDtypeStruct((n, D_MODEL), BF16),
                   jax.ShapeDtypeStruct((n * N_HEADS, HEAD_DIM), F32),
                   jax.ShapeDtypeStruct((n, D_MODEL), BF16),
                   jax.ShapeDtypeStruct((n, D_MODEL), BF16),
                   jax.ShapeDtypeStruct((n, N_HEADS), F32)],
        compiler_params=_params(("parallel",), w_bytes + io_bytes + tmp_bytes),
        name="attn_inproj",
    )(x, gn, wq, wk, wv, wg, wf, bf, gq, gk)


def _cumsum_kernel(x_ref, u_ref, o_ref):
    rows, length = x_ref.shape
    u = u_ref[...]
    carry = jnp.zeros((rows, 1), F32)
    for j in range(length // LANES):
        sl = slice(j * LANES, (j + 1) * LANES)
        x = x_ref[:, sl]
        hi = x.astype(BF16)
        r1 = x - hi.astype(F32)
        mid = r1.astype(BF16)
        lo = (r1 - mid.astype(F32)).astype(BF16)
        c = (_dot(hi, u) + _dot(mid, u)) + _dot(lo, u) + carry
        o_ref[:, sl] = c * LOG2E
        carry = c[:, LANES - 1:LANES]


def _cumsum_lanes(x):
    rows, length = x.shape
    assert length % LANES == 0 and rows % 8 == 0
    idx = jnp.arange(LANES)
    u = (idx[:, None] <= idx[None, :]).astype(BF16)
    return pl.pallas_call(
        _cumsum_kernel,
        out_shape=jax.ShapeDtypeStruct((rows, length), F32),
        compiler_params=_params((), 8 * rows * length * 4 + (1 << 22)),
        name="cumsum_lanes",
    )(x, u)


def _causal_mask(blk):
    r = lax.broadcasted_iota(jnp.int32, (blk, blk), 0)
    c = lax.broadcasted_iota(jnp.int32, (blk, blk), 1)
    return c <= r


def _flash_online_kernel(q_ref, k_ref, v_ref, c_ref, o_ref, *, blk):
    i = pl.program_id(2)
    q = q_ref[...]

    def step(j, carry, masked):
        m, l, acc = carry
        off = pl.multiple_of(j * blk, blk)
        kb = k_ref[pl.ds(off, blk), :]
        vb = v_ref[pl.ds(off, blk), :]
        s = _dot_nt(q, kb) - c_ref[j]
        if masked:
            s = jnp.where(_causal_mask(blk), s, NEG_BIG)
        m_new = jnp.maximum(m, jnp.max(s, axis=-1, keepdims=True))
        p = jnp.exp2(s - m_new)
        alpha = jnp.exp2(m - m_new)
        l = alpha * l + jnp.sum(p, axis=-1, keepdims=True)
        acc = alpha * acc + _dot(p.astype(BF16), vb)
        return m_new, l, acc

    init = (jnp.full((blk, 1), NEG_BIG, F32), jnp.zeros((blk, 1), F32), jnp.zeros((blk, HEAD_DIM), F32))
    carry = lax.fori_loop(0, i, lambda j, c: step(j, c, False), init)
    m, l, acc = step(i, carry, True)
    o_ref[...] = (acc / l).astype(o_ref.dtype)


def _flash_bounded_kernel(q_ref, k_ref, v_ref, c_ref, o_ref, vones_ref, qbias_ref, p_ref, acc_ref, *, tk):
    i = pl.program_id(2)

    @pl.when(i == 0)
    def _():
        lane = lax.broadcasted_iota(jnp.int32, (v_ref.shape[0], LANES), 1)
        vones_ref[:, :HEAD_DIM] = v_ref[...]
        vones_ref[:, HEAD_DIM:] = jnp.where(lane == 0, 1.0, 0.0).astype(BF16)

    causal = _causal_mask(tk)

    def query_bias(row_block):
        lanes = jnp.transpose(jnp.broadcast_to(c_ref[row_block], (LANES, tk)))
        return jnp.concatenate([lanes] * (tk // LANES), axis=1)

    c_top = query_bias(2 * i)
    c_bot = query_bias(2 * i + 1)
    qbias_ref[0, :tk, :] = c_top
    qbias_ref[0, tk:, :] = c_bot
    qbias_ref[1, :tk, :] = jnp.where(causal, c_top, NEG_BIG)
    qbias_ref[1, tk:, :] = c_bot

    def keys(ref, j):
        return ref[pl.ds(pl.multiple_of(j * tk, tk), tk), :]

    def probs(j, bias):
        z = (_dot_nt(q_ref[...], keys(k_ref, j)) + bias) - c_ref[j]
        return jnp.exp2(z).astype(BF16)

    def accumulate(slot, j):
        acc_ref[...] += _dot(p_ref[slot], keys(vones_ref, j))

    first = 2 * i
    p_ref[0] = probs(0, qbias_ref[(i == 0).astype(jnp.int32)])
    acc_ref[...] = jnp.zeros_like(acc_ref)

    def trip(t, _):
        accumulate(0, 2 * t)
        p_ref[1] = probs(2 * t + 1, qbias_ref[0])
        accumulate(1, 2 * t + 1)
        p_ref[0] = probs(2 * t + 2, qbias_ref[(2 * t + 2 == first).astype(jnp.int32)])
        return 0

    lax.fori_loop(0, i, trip, 0)
    accumulate(0, first)
    z = (_dot_nt(q_ref[tk:, :], keys(k_ref, first + 1)) + c_bot) - c_ref[first + 1]
    p_last = jnp.exp2(jnp.where(causal, z, NEG_BIG)).astype(BF16)
    acc_ref[tk:, :] += _dot(p_last, keys(vones_ref, first + 1))
    acc = acc_ref[...]
    o_ref[...] = (acc[:, :HEAD_DIM] / acc[:, HEAD_DIM:HEAD_DIM + 1]).astype(o_ref.dtype)


def _flash_prompt(q, kb, vb, c, batch, seq, tk, bounded):
    tq = 2 * tk if bounded else tk
    nq = seq // tq
    assert seq % tq == 0
    qspec = pl.BlockSpec((tq, HEAD_DIM), lambda b, h, i: (b * nq + i, h))
    kvspec = pl.BlockSpec((seq, HEAD_DIM), lambda b, h, i: (b, h))
    cspec = pl.BlockSpec((None, seq // tk, 1, tk), lambda b, h, i: (b * N_HEADS + h, 0, 0, 0))
    vmem = 2 * 2 * seq * HEAD_DIM * 2 + 4 * tq * HEAD_DIM * 2 + 2 * seq * 4 + 8 * tq * tk * 4
    if bounded:
        body = functools.partial(_flash_bounded_kernel, tk=tk)
        scratch = [pltpu.VMEM((seq, 2 * HEAD_DIM), BF16),
                   pltpu.VMEM((2, tq, tk), F32),
                   pltpu.VMEM((2, tq, tk), BF16),
                   pltpu.VMEM((tq, 2 * HEAD_DIM), F32)]
        vmem += seq * 2 * HEAD_DIM * 2 + 2 * tq * tk * 4 + 2 * tq * tk * 2 + tq * 2 * HEAD_DIM * 4
    else:
        body = functools.partial(_flash_online_kernel, blk=tk)
        scratch = []
    return pl.pallas_call(
        body,
        grid=(batch, N_HEADS, nq),
        in_specs=[qspec, kvspec, kvspec, cspec],
        out_specs=qspec,
        out_shape=jax.ShapeDtypeStruct(q.shape, BF16),
        scratch_shapes=scratch,
        compiler_params=_params(("parallel", "parallel", "arbitrary"), vmem),
        name="flash_bounded" if bounded else "flash_online",
    )(q, kb, vb, c)


def _sample_attn_kernel(q_ref, ck_ref, cv_ref, kn_ref, vn_ref, cc_ref, cn_ref, o_ref, m_ref, l_ref, acc_ref,
                        *, chunk, t_new):
    step = pl.program_id(1)

    @pl.when(step == 0)
    def _():
        m_ref[...] = jnp.full_like(m_ref, NEG_BIG)
        l_ref[...] = jnp.zeros_like(l_ref)
        acc_ref[...] = jnp.zeros_like(acc_ref)

    def update(h, s, values):
        m_old = m_ref[h]
        m_new = jnp.maximum(m_old, jnp.max(s, axis=-1, keepdims=True))
        p = jnp.exp2(s - m_new)
        alpha = jnp.exp2(m_old - m_new)
        l_ref[h] = alpha * l_ref[h] + jnp.sum(p, axis=-1, keepdims=True)
        acc_ref[h] = alpha * acc_ref[h] + _dot(p.astype(BF16), values)
        m_ref[h] = m_new

    for h in range(N_HEADS):
        sl = slice(h * HEAD_DIM, (h + 1) * HEAD_DIM)
        rows = pl.ds(h, chunk, stride=N_HEADS)
        s = _dot_nt(q_ref[:, sl], ck_ref[rows, :].astype(BF16)) - cc_ref[h:h + 1, :]
        update(h, s, cv_ref[rows, :].astype(BF16))

    @pl.when(step == pl.num_programs(1) - 1)
    def _():
        r = lax.broadcasted_iota(jnp.int32, (t_new, t_new), 0)
        col = lax.broadcasted_iota(jnp.int32, (t_new, t_new), 1)
        for h in range(N_HEADS):
            sl = slice(h * HEAD_DIM, (h + 1) * HEAD_DIM)
            s = _dot_nt(q_ref[:, sl], kn_ref[:, sl]) - cn_ref[h:h + 1, :t_new]
            update(h, jnp.where(col <= r, s, NEG_BIG), vn_ref[:, sl])
            o_ref[:, sl] = (acc_ref[h] / l_ref[h]).astype(o_ref.dtype)


def _sample_attn(q, cache_k, cache_v, kb, vb, c_cache, c_new, batch, t_new, past, chunk):
    assert past % chunk == 0
    nspec = pl.BlockSpec((t_new, D_MODEL), lambda b, s: (b, 0))
    cachespec = pl.BlockSpec((None, chunk * N_HEADS, HEAD_DIM), lambda b, s: (b, s, 0))
    ccspec = pl.BlockSpec((None, None, N_HEADS, chunk), lambda b, s: (b, s, 0, 0))
    cnspec = pl.BlockSpec((None, N_HEADS, LANES), lambda b, s: (b, 0, 0))
    head_rows = (batch, past * N_HEADS, HEAD_DIM)
    vmem = 2 * 2 * chunk * N_HEADS * HEAD_DIM * 4 + 8 * chunk * HEAD_DIM * 4 + (1 << 22)
    return pl.pallas_call(
        functools.partial(_sample_attn_kernel, chunk=chunk, t_new=t_new),
        grid=(batch, past // chunk),
        in_specs=[nspec, cachespec, cachespec, nspec, nspec, ccspec, cnspec],
        out_specs=nspec,
        out_shape=jax.ShapeDtypeStruct(q.shape, BF16),
        scratch_shapes=[pltpu.VMEM((N_HEADS, t_new, 1), F32), pltpu.VMEM((N_HEADS, t_new, 1), F32),
                        pltpu.VMEM((N_HEADS, t_new, HEAD_DIM), F32)],
        compiler_params=_params(("parallel", "arbitrary"), vmem),
        name="sample_attn",
    )(q, cache_k.reshape(head_rows), cache_v.reshape(head_rows), kb, vb, c_cache, c_new)


def _attn_outproj_kernel(x_ref, g_ref, o_ref, w_ref, y_ref):
    gated = _silu(g_ref[...].astype(F32)) * o_ref[...].astype(F32)
    y_ref[...] = x_ref[...] + _dot(gated.astype(BF16), w_ref[...])


def _attn_outproj(x, gate, o, w, tm):
    n = x.shape[0]
    assert n % tm == 0
    row = pl.BlockSpec((tm, D_MODEL), lambda i: (i, 0))
    vmem = 2 * tm * D_MODEL * (4 + 2 + 2 + 4) + 4 * D_MODEL * D_MODEL + 4 * tm * D_MODEL * 4
    return pl.pallas_call(
        _attn_outproj_kernel,
        grid=(n // tm,),
        in_specs=[row, row, row, pl.BlockSpec(w.shape, lambda i: (0, 0))],
        out_specs=row,
        out_shape=jax.ShapeDtypeStruct(x.shape, F32),
        compiler_params=_params(("parallel",), vmem),
        name="attn_outproj",
    )(x, gate, o, w)


def _conv_layer_kernel(x_ref, hist_ref, gn_ref, wb_ref, wc_ref, wu_ref, wg_ref, cw_ref, wo_ref,
                       y_ref, st_ref, tail_ref, *, tm):
    @pl.when(pl.program_id(1) == 0)
    def _():
        tail_ref[...] = jnp.zeros_like(tail_ref)
        tail_ref[8 - (CONV_K - 1):, :] = hist_ref[...]

    x = x_ref[...]
    hb = (x * _rms_scale(x) * gn_ref[...]).astype(BF16)
    cu = _dot(hb, wc_ref[...]) * _dot(hb, wu_ref[...])

    tail = tail_ref[...]
    h_m2 = tail[6:7, :]
    h_m1 = tail[7:8, :]
    r = lax.broadcasted_iota(jnp.int32, cu.shape, 0)
    s1 = jnp.where(r == 0, h_m1, pltpu.roll(cu, 1, 0))
    s2 = jnp.where(r == 0, h_m2, jnp.where(r == 1, h_m1, pltpu.roll(cu, 2, 0)))
    cw = cw_ref[...]
    conv = cw[0:1, :] * s2 + cw[1:2, :] * s1 + cw[2:3, :] * cu

    tail_ref[...] = cu[tm - 8:, :]
    st_ref[...] = cu[tm - (CONV_K - 1):, :]

    mixed = _silu(_dot(hb, wg_ref[...])) * _dot(hb, wb_ref[...]) * conv
    y_ref[...] = x + _dot(mixed.astype(BF16), wo_ref[...])


def _conv_layer(x, hist, gn, wb, wc, wu, wg, cw, wo, batch, seq, tm):
    assert seq % tm == 0 and tm % 8 == 0 and tm >= 8
    nt = seq // tm
    row = pl.BlockSpec((tm, D_MODEL), lambda b, t: (b * nt + t, 0))
    st = pl.BlockSpec((None, CONV_K - 1, D_MODEL), lambda b, t: (b, 0, 0))
    full = lambda a: pl.BlockSpec(a.shape, lambda b, t: (0, 0))
    vmem = 2 * 2 * 5 * D_MODEL * D_MODEL + 4 * tm * D_MODEL * 4 + 8 * tm * D_MODEL * 4 + (1 << 21)
    return pl.pallas_call(
        functools.partial(_conv_layer_kernel, tm=tm),
        grid=(batch, nt),
        in_specs=[row, st, full(gn), full(wb), full(wc), full(wu), full(wg), full(cw), full(wo)],
        out_specs=[row, st],
        out_shape=[jax.ShapeDtypeStruct(x.shape, F32),
                   jax.ShapeDtypeStruct((batch, CONV_K - 1, D_MODEL), F32)],
        scratch_shapes=[pltpu.VMEM((8, D_MODEL), F32)],
        compiler_params=_params(("parallel", "arbitrary"), vmem),
        name="conv_layer",
    )(x, hist, gn, wb, wc, wu, wg, cw, wo)


def _row(v, width=None):
    v = v.astype(F32).reshape(1, -1)
    if width is not None and v.shape[1] < width:
        v = jnp.pad(v, ((0, 0), (0, width - v.shape[1])))
    return v


def _attn_weights(g_norm, w_in, b_f, g_q, g_k, w_out):
    aw = N_HEADS * HEAD_DIM
    wq, wk, wv, wg = (w_in[:, i * aw:(i + 1) * aw].astype(BF16) for i in range(4))
    wf = jnp.pad(w_in[:, 4 * aw:], ((0, 0), (0, LANES - N_HEADS))).astype(BF16)
    return (_row(g_norm), wq, wk, wv, wg, wf, _row(b_f, LANES), _row(g_q), _row(g_k)), w_out.astype(BF16)


def _heads_to_rows(lf, batch, length):
    return jnp.transpose(lf.reshape(batch, length, N_HEADS), (0, 2, 1)).reshape(batch * N_HEADS, length)


def _attn_prompt_layer(x, params, batch, seq, tm, tk):
    inw, wo = _attn_weights(*params)
    q, k, kb, v, vb, gate, lf = _attn_inproj(x, *inw, tm=tm)
    c = _cumsum_lanes(_heads_to_rows(lf, batch, seq))
    c = c.reshape(batch * N_HEADS, seq // tk, 1, tk)
    g_q, g_k = params[3], params[4]
    logit_bound = 1.01 * HEAD_DIM ** 0.5 * jnp.max(jnp.abs(g_q)) * jnp.max(jnp.abs(g_k))
    o = lax.cond(logit_bound < FAST_LOGIT_BOUND,
                 functools.partial(_flash_prompt, batch=batch, seq=seq, tk=tk, bounded=True),
                 functools.partial(_flash_prompt, batch=batch, seq=seq, tk=tk, bounded=False),
                 q, kb, vb, c)
    y = _attn_outproj(x, gate, o, wo, tm=tm)
    return y, k, v, lf


def _attn_sample_layer(x, cache_k, cache_v, cache_logf, params, batch, t_new):
    past = cache_k.shape[1]
    inw, wo = _attn_weights(*params)
    q, k, kb, v, vb, gate, lf = _attn_inproj(x, *inw, tm=x.shape[0])
    lf_all = jnp.concatenate([cache_logf.astype(F32), lf.reshape(batch, t_new, N_HEADS)], axis=1)
    total = past + t_new
    padded = -(-total // LANES) * LANES
    lf_rows = _heads_to_rows(lf_all.reshape(batch * total, N_HEADS), batch, total)
    lf_rows = jnp.pad(lf_rows, ((0, 0), (0, padded - total)))
    c = _cumsum_lanes(lf_rows).reshape(batch, N_HEADS, padded)
    chunk = 1024
    c_cache = jnp.transpose(c[:, :, :past].reshape(batch, N_HEADS, past // chunk, chunk), (0, 2, 1, 3))
    c_new = c[:, :, past:past + LANES]
    o = _sample_attn(q, cache_k, cache_v, kb, vb, c_cache, c_new, batch, t_new, past, chunk)
    y = _attn_outproj(x, gate, o, wo, tm=x.shape[0])
    return y, k, v, lf


def _conv_weights(g_norm, w_in, conv_w, w_out):
    w = D_MODEL
    wb, wc, wu, wg = (w_in[:, i * w:(i + 1) * w].astype(BF16) for i in range(4))
    return _row(g_norm), wb, wc, wu, wg, conv_w.astype(F32), w_out.astype(BF16)


def kernel(x_prompt, x_sample, cache_k_l0, cache_v_l0, cache_logf_l0, state_conv_l1, cache_k_l2, cache_v_l2, cache_logf_l2, state_conv_l3, norm_l0, w_in_l0, b_f_l0, qnorm_l0, knorm_l0, w_out_l0, norm_l1, w_in_l1, conv_w_l1, w_out_l1, norm_l2, w_in_l2, b_f_l2, qnorm_l2, knorm_l2, w_out_l2, norm_l3, w_in_l3, conv_w_l3, w_out_l3):
    batch, seq, d = x_prompt.shape
    dec_batch, dec_seq, _ = x_sample.shape
    caches = [(cache_k_l0, cache_v_l0, cache_logf_l0), (state_conv_l1,),
              (cache_k_l2, cache_v_l2, cache_logf_l2), (state_conv_l3,)]
    params = [(norm_l0, w_in_l0, b_f_l0, qnorm_l0, knorm_l0, w_out_l0),
              (norm_l1, w_in_l1, conv_w_l1, w_out_l1),
              (norm_l2, w_in_l2, b_f_l2, qnorm_l2, knorm_l2, w_out_l2),
              (norm_l3, w_in_l3, conv_w_l3, w_out_l3)]
    tm, tk = 512, 512
    yp = x_prompt.reshape(batch * seq, d)
    ys = x_sample.reshape(dec_batch * dec_seq, d)
    outs = []
    for i in range(4):
        if i % 2 == 0:
            yp, kp, vp, lfp = _attn_prompt_layer(yp, params[i], batch, seq, tm, tk)
            ys, ks, vs, lfs = _attn_sample_layer(ys, *caches[i], params[i], dec_batch, dec_seq)
            outs += [kp.reshape(batch, seq, N_HEADS, HEAD_DIM), vp.reshape(batch, seq, N_HEADS, HEAD_DIM),
                     lfp.reshape(batch, seq, N_HEADS),
                     ks.reshape(dec_batch, dec_seq, N_HEADS, HEAD_DIM),
                     vs.reshape(dec_batch, dec_seq, N_HEADS, HEAD_DIM),
                     lfs.reshape(dec_batch, dec_seq, N_HEADS)]
        else:
            cw = _conv_weights(*params[i])
            zero_hist = jnp.zeros((batch, CONV_K - 1, d), F32)
            yp, cp = _conv_layer(yp, zero_hist, *cw, batch=batch, seq=seq, tm=tm)
            ys, cs = _conv_layer(ys, caches[i][0].astype(F32), *cw, batch=dec_batch, seq=dec_seq, tm=dec_seq)
            outs += [cp, cs]
    return (yp.reshape(batch, seq, d), ys.reshape(dec_batch, dec_seq, d), *outs)
```

```python
import functools

import jax
import jax.numpy as jnp
from jax import lax
from jax.experimental import pallas as pl
from jax.experimental.pallas import tpu as pltpu

N_HEADS = 8
HEAD_DIM = 128
D_MODEL = 1024
CONV_K = 3
NORM_EPS = 1e-6
LOG2E = 1.4426950408889634
QK_SCALE_LOG2 = HEAD_DIM ** -0.5 * LOG2E
NEG_BIG = -1e30
FAST_LOGIT_BOUND = 60.0

LANES = 128
V7X_SCOPED_VMEM_BYTES = 60000 * 1024

F32 = jnp.float32
BF16 = jnp.bfloat16


def _params(semantics, vmem_bytes):
    return pltpu.CompilerParams(dimension_semantics=semantics,
                                vmem_limit_bytes=min(int(vmem_bytes), V7X_SCOPED_VMEM_BYTES))


def _rms_scale(x):
    return lax.rsqrt(jnp.mean(x * x, axis=-1, keepdims=True) + NORM_EPS)


def _dot(a, b):
    return jnp.dot(a, b, preferred_element_type=F32)


def _dot_nt(a, b):
    return lax.dot_general(a, b, (((1,), (1,)), ((), ())), preferred_element_type=F32)


def _silu(g):
    return g / (1.0 + jnp.exp(-g))


def _attn_inproj_kernel(x_ref, gn_ref, wq_ref, wk_ref, wv_ref, wg_ref, wf_ref, bf_ref, gq_ref, gk_ref,
                        q_ref, k_ref, kb_ref, v_ref, vb_ref, g_ref, lf_ref):
    x = x_ref[...]
    hb = (x * _rms_scale(x) * gn_ref[...]).astype(BF16)

    tm = x.shape[0]
    zq = _dot(hb, wq_ref[...])
    zk = _dot(hb, wk_ref[...])
    zv = _dot(hb, wv_ref[...])
    gq = gq_ref[...] * QK_SCALE_LOG2
    gk = gk_ref[...]
    for h in range(N_HEADS):
        sl = slice(h * HEAD_DIM, (h + 1) * HEAD_DIM)
        rows = pl.ds(h, tm, stride=N_HEADS)
        qh = zq[:, sl]
        q_ref[:, sl] = (qh * _rms_scale(qh) * gq).astype(BF16)
        kh = zk[:, sl]
        kn = kh * _rms_scale(kh) * gk
        k_ref[rows, :] = kn
        kb_ref[:, sl] = kn.astype(BF16)
        v_ref[rows, :] = zv[:, sl]

    vb_ref[...] = zv.astype(BF16)
    g_ref[...] = _dot(hb, wg_ref[...]).astype(BF16)

    zf = _dot(hb, wf_ref[...]) + bf_ref[...]
    lf = jnp.minimum(zf, 0.0) - jnp.log1p(jnp.exp(-jnp.abs(zf)))
    lf_ref[...] = lf[:, :N_HEADS]


def _attn_inproj(x, gn, wq, wk, wv, wg, wf, bf, gq, gk, tm):
    n = x.shape[0]
    assert n % tm == 0
    row = lambda c: pl.BlockSpec((tm, c), lambda i: (i, 0))
    head_rows = pl.BlockSpec((tm * N_HEADS, HEAD_DIM), lambda i: (i, 0))
    full = lambda a: pl.BlockSpec(a.shape, lambda i: (0, 0))
    w_bytes = 2 * 2 * (4 * D_MODEL * D_MODEL + D_MODEL * LANES)
    io_bytes = 2 * tm * D_MODEL * (4 + 2 + 4 + 2 + 4 + 2 + 2)
    tmp_bytes = 6 * tm * D_MODEL * 4
    return pl.pallas_call(
        _attn_inproj_kernel,
        grid=(n // tm,),
        in_specs=[row(D_MODEL), full(gn), full(wq), full(wk), full(wv), full(wg), full(wf), full(bf),
                  full(gq), full(gk)],
        out_specs=[row(D_MODEL), head_rows, row(D_MODEL), head_rows, row(D_MODEL), row(D_MODEL),
                   row(N_HEADS)],
        out_shape=[jax.ShapeDtypeStruct((n, D_MODEL), BF16),
                   jax.ShapeDtypeStruct((n * N_HEADS, HEAD_DIM), F32),
                   jax.ShapeDtypeStruct((n, D_MODEL), BF16),
                   jax.ShapeDtypeStruct((n * N_HEADS, HEAD_DIM), F32),
                   jax.ShapeDtypeStruct((n, D_MODEL), BF16),
                   jax.ShapeDtypeStruct((n, D_MODEL), BF16),
                   jax.ShapeDtypeStruct((n, N_HEADS), F32)],
        compiler_params=_params(("parallel",), w_bytes + io_bytes + tmp_bytes),
        name="attn_inproj",
    )(x, gn, wq, wk, wv, wg, wf, bf, gq, gk)


def _cumsum_kernel(x_ref, u_ref, o_ref):
    rows, length = x_ref.shape
    u = u_ref[...]
    carry = jnp.zeros((rows, 1), F32)
    for j in range(length // LANES):
        sl = slice(j * LANES, (j + 1) * LANES)
        x = x_ref[:, sl]
        hi = x.astype(BF16)
        r1 = x - hi.astype(F32)
        mid = r1.astype(BF16)
        lo = (r1 - mid.astype(F32)).astype(BF16)
        c = (_dot(hi, u) + _dot(mid, u)) + _dot(lo, u) + carry
        o_ref[:, sl] = c * LOG2E
        carry = c[:, LANES - 1:LANES]


def _cumsum_lanes(x):
    rows, length = x.shape
    assert length % LANES == 0 and rows % 8 == 0
    idx = jnp.arange(LANES)
    u = (idx[:, None] <= idx[None, :]).astype(BF16)
    return pl.pallas_call(
        _cumsum_kernel,
        out_shape=jax.ShapeDtypeStruct((rows, length), F32),
        compiler_params=_params((), 8 * rows * length * 4 + (1 << 22)),
        name="cumsum_lanes",
    )(x, u)


def _causal_mask(blk):
    r = lax.broadcasted_iota(jnp.int32, (blk, blk), 0)
    c = lax.broadcasted_iota(jnp.int32, (blk, blk), 1)
    return c <= r


def _flash_online_kernel(q_ref, k_ref, v_ref, c_ref, o_ref, *, blk):
    i = pl.program_id(2)
    q = q_ref[...]

    def step(j, carry, masked):
        m, l, acc = carry
        off = pl.multiple_of(j * blk, blk)
        kb = k_ref[pl.ds(off, blk), :]
        vb = v_ref[pl.ds(off, blk), :]
        s = _dot_nt(q, kb) - c_ref[j]
        if masked:
            s = jnp.where(_causal_mask(blk), s, NEG_BIG)
        m_new = jnp.maximum(m, jnp.max(s, axis=-1, keepdims=True))
        p = jnp.exp2(s - m_new)
        alpha = jnp.exp2(m - m_new)
        l = alpha * l + jnp.sum(p, axis=-1, keepdims=True)
        acc = alpha * acc + _dot(p.astype(BF16), vb)
        return m_new, l, acc

    init = (jnp.full((blk, 1), NEG_BIG, F32), jnp.zeros((blk, 1), F32), jnp.zeros((blk, HEAD_DIM), F32))
    carry = lax.fori_loop(0, i, lambda j, c: step(j, c, False), init)
    m, l, acc = step(i, carry, True)
    o_ref[...] = (acc / l).astype(o_ref.dtype)


VT_ROWS = HEAD_DIM + 16


def _flash_bounded_kernel(q_ref, k_ref, v_ref, c_ref, o_ref, vt_ref, cs_ref, mask_ref, p_ref, acc_ref, *, tk):
    i = pl.program_id(2)
    tq = 2 * tk
    lane_tiles = tk // LANES

    @pl.when(i == 0)
    def _():
        sub = lax.broadcasted_iota(jnp.int32, (VT_ROWS - HEAD_DIM, tk), 0)
        ones_rows = jnp.where(sub == 0, 1.0, 0.0).astype(BF16)
        for jb in range(v_ref.shape[0] // tk):
            vt_ref[jb, :HEAD_DIM, :] = jnp.transpose(v_ref[jb * tk:(jb + 1) * tk, :].astype(F32)).astype(BF16)
            vt_ref[jb, HEAD_DIM:, :] = ones_rows
            c_row = c_ref[jb]
            for n in range(lane_tiles):
                lanes = slice(n * LANES, (n + 1) * LANES)
                cs_ref[jb, lanes, :] = jnp.transpose(jnp.broadcast_to(c_row[:, lanes], (LANES, LANES)))
        key = lax.broadcasted_iota(jnp.int32, (tk, tq), 0)
        query = lax.broadcasted_iota(jnp.int32, (tk, tq), 1)
        mask_ref[0] = jnp.zeros((tk, tq), F32)
        mask_ref[1] = jnp.where(key <= query, 0.0, NEG_BIG)

    c_t = jnp.concatenate([c_ref[2 * i], c_ref[2 * i + 1]], axis=1)

    def keys(j):
        return k_ref[pl.ds(pl.multiple_of(j * tk, tk), tk), :]

    def logits_t(j, cols):
        c_s = jnp.concatenate([cs_ref[j]] * lane_tiles, axis=1)
        return (_dot_nt(keys(j), q_ref[cols, :]) + c_t[:, cols]) - c_s

    def probs_t(slot, j, mask_index):
        for half in range(2):
            cols = slice(half * tk, (half + 1) * tk)
            z = logits_t(j, cols)
            if mask_index is not None:
                z = z + mask_ref[mask_index, :, cols]
            p_ref[slot, :, cols] = jnp.exp2(z).astype(BF16)

    def accumulate(slot, j):
        acc_ref[...] += _dot(vt_ref[j], p_ref[slot])

    first = 2 * i
    probs_t(0, 0, (i == 0).astype(jnp.int32))
    acc_ref[...] = jnp.zeros_like(acc_ref)

    def trip(t, _):
        accumulate(0, 2 * t)
        probs_t(1, 2 * t + 1, None)
        accumulate(1, 2 * t + 1)
        probs_t(0, 2 * t + 2, (2 * t + 2 == first).astype(jnp.int32))
        return 0

    lax.fori_loop(0, i, trip, 0)
    accumulate(0, first)
    z = logits_t(first + 1, slice(tk, tq)) + mask_ref[1, :, :tk]
    acc_ref[:, tk:] += _dot(vt_ref[first + 1], jnp.exp2(z).astype(BF16))
    acc = acc_ref[...]
    o_ref[...] = jnp.transpose(acc[:HEAD_DIM, :] / acc[HEAD_DIM:HEAD_DIM + 1, :]).astype(o_ref.dtype)


def _flash_prompt(q, kb, vb, c, batch, seq, tk, bounded):
    tq = 2 * tk if bounded else tk
    nq = seq // tq
    assert seq % tq == 0
    qspec = pl.BlockSpec((tq, HEAD_DIM), lambda b, h, i: (b * nq + i, h))
    kvspec = pl.BlockSpec((seq, HEAD_DIM), lambda b, h, i: (b, h))
    cspec = pl.BlockSpec((None, seq // tk, 1, tk), lambda b, h, i: (b * N_HEADS + h, 0, 0, 0))
    vmem = 2 * 2 * seq * HEAD_DIM * 2 + 4 * tq * HEAD_DIM * 2 + 2 * seq * 4 + 8 * tq * tk * 4
    if bounded:
        body = functools.partial(_flash_bounded_kernel, tk=tk)
        scratch = [pltpu.VMEM((seq // tk, VT_ROWS, tk), BF16),
                   pltpu.VMEM((seq // tk, tk, LANES), F32),
                   pltpu.VMEM((2, tk, tq), F32),
                   pltpu.VMEM((2, tk, tq), BF16),
                   pltpu.VMEM((VT_ROWS, tq), F32)]
        vmem += seq * VT_ROWS * 2 + seq * LANES * 4 + 2 * tq * tk * 4 + 2 * tq * tk * 2 + tq * VT_ROWS * 4
    else:
        body = functools.partial(_flash_online_kernel, blk=tk)
        scratch = []
    return pl.pallas_call(
        body,
        grid=(batch, N_HEADS, nq),
        in_specs=[qspec, kvspec, kvspec, cspec],
        out_specs=qspec,
        out_shape=jax.ShapeDtypeStruct(q.shape, BF16),
        scratch_shapes=scratch,
        compiler_params=_params(("parallel", "parallel", "arbitrary"), vmem),
        name="flash_bounded" if bounded else "flash_online",
    )(q, kb, vb, c)


def _sample_attn_kernel(q_ref, ck_ref, cv_ref, kn_ref, vn_ref, cc_ref, cn_ref, o_ref, m_ref, l_ref, acc_ref,
                        *, chunk, t_new):
    step = pl.program_id(1)

    @pl.when(step == 0)
    def _():
        m_ref[...] = jnp.full_like(m_ref, NEG_BIG)
        l_ref[...] = jnp.zeros_like(l_ref)
        acc_ref[...] = jnp.zeros_like(acc_ref)

    def update(h, s, values):
        m_old = m_ref[h]
        m_new = jnp.maximum(m_old, jnp.max(s, axis=-1, keepdims=True))
        p = jnp.exp2(s - m_new)
        alpha = jnp.exp2(m_old - m_new)
        l_ref[h] = alpha * l_ref[h] + jnp.sum(p, axis=-1, keepdims=True)
        acc_ref[h] = alpha * acc_ref[h] + _dot(p.astype(BF16), values)
        m_ref[h] = m_new

    for h in range(N_HEADS):
        sl = slice(h * HEAD_DIM, (h + 1) * HEAD_DIM)
        rows = pl.ds(h, chunk, stride=N_HEADS)
        s = _dot_nt(q_ref[:, sl], ck_ref[rows, :].astype(BF16)) - cc_ref[h:h + 1, :]
        update(h, s, cv_ref[rows, :].astype(BF16))

    @pl.when(step == pl.num_programs(1) - 1)
    def _():
        r = lax.broadcasted_iota(jnp.int32, (t_new, t_new), 0)
        col = lax.broadcasted_iota(jnp.int32, (t_new, t_new), 1)
        for h in range(N_HEADS):
            sl = slice(h * HEAD_DIM, (h + 1) * HEAD_DIM)
            s = _dot_nt(q_ref[:, sl], kn_ref[:, sl]) - cn_ref[h:h + 1, :t_new]
            update(h, jnp.where(col <= r, s, NEG_BIG), vn_ref[:, sl])
            o_ref[:, sl] = (acc_ref[h] / l_ref[h]).astype(o_ref.dtype)


def _sample_attn(q, cache_k, cache_v, kb, vb, c_cache, c_new, batch, t_new, past, chunk):
    assert past % chunk == 0
    nspec = pl.BlockSpec((t_new, D_MODEL), lambda b, s: (b, 0))
    cachespec = pl.BlockSpec((None, chunk * N_HEADS, HEAD_DIM), lambda b, s: (b, s, 0))
    ccspec = pl.BlockSpec((None, None, N_HEADS, chunk), lambda b, s: (b, s, 0, 0))
    cnspec = pl.BlockSpec((None, N_HEADS, LANES), lambda b, s: (b, 0, 0))
    head_rows = (batch, past * N_HEADS, HEAD_DIM)
    vmem = 2 * 2 * chunk * N_HEADS * HEAD_DIM * 4 + 8 * chunk * HEAD_DIM * 4 + (1 << 22)
    return pl.pallas_call(
        functools.partial(_sample_attn_kernel, chunk=chunk, t_new=t_new),
        grid=(batch, past // chunk),
        in_specs=[nspec, cachespec, cachespec, nspec, nspec, ccspec, cnspec],
        out_specs=nspec,
        out_shape=jax.ShapeDtypeStruct(q.shape, BF16),
        scratch_shapes=[pltpu.VMEM((N_HEADS, t_new, 1), F32), pltpu.VMEM((N_HEADS, t_new, 1), F32),
                        pltpu.VMEM((N_HEADS, t_new, HEAD_DIM), F32)],
        compiler_params=_params(("parallel", "arbitrary"), vmem),
        name="sample_attn",
    )(q, cache_k.reshape(head_rows), cache_v.reshape(head_rows), kb, vb, c_cache, c_new)


def _attn_outproj_kernel(x_ref, g_ref, o_ref, w_ref, y_ref):
    gated = _silu(g_ref[...].astype(F32)) * o_ref[...].astype(F32)
    y_ref[...] = x_ref[...] + _dot(gated.astype(BF16), w_ref[...])


def _attn_outproj(x, gate, o, w, tm):
    n = x.shape[0]
    assert n % tm == 0
    row = pl.BlockSpec((tm, D_MODEL), lambda i: (i, 0))
    vmem = 2 * tm * D_MODEL * (4 + 2 + 2 + 4) + 4 * D_MODEL * D_MODEL + 4 * tm * D_MODEL * 4
    return pl.pallas_call(
        _attn_outproj_kernel,
        grid=(n // tm,),
        in_specs=[row, row, row, pl.BlockSpec(w.shape, lambda i: (0, 0))],
        out_specs=row,
        out_shape=jax.ShapeDtypeStruct(x.shape, F32),
        compiler_params=_params(("parallel",), vmem),
        name="attn_outproj",
    )(x, gate, o, w)


def _conv_layer_kernel(x_ref, hist_ref, gn_ref, wb_ref, wc_ref, wu_ref, wg_ref, cw_ref, wo_ref,
                       y_ref, st_ref, tail_ref, *, tm):
    @pl.when(pl.program_id(1) == 0)
    def _():
        tail_ref[...] = jnp.zeros_like(tail_ref)
        tail_ref[8 - (CONV_K - 1):, :] = hist_ref[...]

    x = x_ref[...]
    hb = (x * _rms_scale(x) * gn_ref[...]).astype(BF16)
    cu = _dot(hb, wc_ref[...]) * _dot(hb, wu_ref[...])

    tail = tail_ref[...]
    h_m2 = tail[6:7, :]
    h_m1 = tail[7:8, :]
    r = lax.broadcasted_iota(jnp.int32, cu.shape, 0)
    s1 = jnp.where(r == 0, h_m1, pltpu.roll(cu, 1, 0))
    s2 = jnp.where(r == 0, h_m2, jnp.where(r == 1, h_m1, pltpu.roll(cu, 2, 0)))
    cw = cw_ref[...]
    conv = cw[0:1, :] * s2 + cw[1:2, :] * s1 + cw[2:3, :] * cu

    tail_ref[...] = cu[tm - 8:, :]
    st_ref[...] = cu[tm - (CONV_K - 1):, :]

    mixed = _silu(_dot(hb, wg_ref[...])) * _dot(hb, wb_ref[...]) * conv
    y_ref[...] = x + _dot(mixed.astype(BF16), wo_ref[...])


def _conv_layer(x, hist, gn, wb, wc, wu, wg, cw, wo, batch, seq, tm):
    assert seq % tm == 0 and tm % 8 == 0 and tm >= 8
    nt = seq // tm
    row = pl.BlockSpec((tm, D_MODEL), lambda b, t: (b * nt + t, 0))
    st = pl.BlockSpec((None, CONV_K - 1, D_MODEL), lambda b, t: (b, 0, 0))
    full = lambda a: pl.BlockSpec(a.shape, lambda b, t: (0, 0))
    vmem = 2 * 2 * 5 * D_MODEL * D_MODEL + 4 * tm * D_MODEL * 4 + 8 * tm * D_MODEL * 4 + (1 << 21)
    return pl.pallas_call(
        functools.partial(_conv_layer_kernel, tm=tm),
        grid=(batch, nt),
        in_specs=[row, st, full(gn), full(wb), full(wc), full(wu), full(wg), full(cw), full(wo)],
        out_specs=[row, st],
        out_shape=[jax.ShapeDtypeStruct(x.shape, F32),
                   jax.ShapeDtypeStruct((batch, CONV_K - 1, D_MODEL), F32)],
        scratch_shapes=[pltpu.VMEM((8, D_MODEL), F32)],
        compiler_params=_params(("parallel", "arbitrary"), vmem),
        name="conv_layer",
    )(x, hist, gn, wb, wc, wu, wg, cw, wo)


def _row(v, width=None):
    v = v.astype(F32).reshape(1, -1)
    if width is not None and v.shape[1] < width:
        v = jnp.pad(v, ((0, 0), (0, width - v.shape[1])))
    return v


def _attn_weights(g_norm, w_in, b_f, g_q, g_k, w_out):
    aw = N_HEADS * HEAD_DIM
    wq, wk, wv, wg = (w_in[:, i * aw:(i + 1) * aw].astype(BF16) for i in range(4))
    wf = jnp.pad(w_in[:, 4 * aw:], ((0, 0), (0, LANES - N_HEADS))).astype(BF16)
    return (_row(g_norm), wq, wk, wv, wg, wf, _row(b_f, LANES), _row(g_q), _row(g_k)), w_out.astype(BF16)


def _heads_to_rows(lf, batch, length):
    return jnp.transpose(lf.reshape(batch, length, N_HEADS), (0, 2, 1)).reshape(batch * N_HEADS, length)


def _attn_prompt_layer(x, params, batch, seq, tm, tk):
    inw, wo = _attn_weights(*params)
    q, k, kb, v, vb, gate, lf = _attn_inproj(x, *inw, tm=tm)
    c = _cumsum_lanes(_heads_to_rows(lf, batch, seq))
    c = c.reshape(batch * N_HEADS, seq // tk, 1, tk)
    g_q, g_k = params[3], params[4]
    logit_bound = 1.01 * HEAD_DIM ** 0.5 * jnp.max(jnp.abs(g_q)) * jnp.max(jnp.abs(g_k))
    o = lax.cond(logit_bound < FAST_LOGIT_BOUND,
                 functools.partial(_flash_prompt, batch=batch, seq=seq, tk=tk, bounded=True),
                 functools.partial(_flash_prompt, batch=batch, seq=seq, tk=tk, bounded=False),
                 q, kb, vb, c)
    y = _attn_outproj(x, gate, o, wo, tm=tm)
    return y, k, v, lf


def _attn_sample_layer(x, cache_k, cache_v, cache_logf, params, batch, t_new):
    past = cache_k.shape[1]
    inw, wo = _attn_weights(*params)
    q, k, kb, v, vb, gate, lf = _attn_inproj(x, *inw, tm=x.shape[0])
    lf_all = jnp.concatenate([cache_logf.astype(F32), lf.reshape(batch, t_new, N_HEADS)], axis=1)
    total = past + t_new
    padded = -(-total // LANES) * LANES
    lf_rows = _heads_to_rows(lf_all.reshape(batch * total, N_HEADS), batch, total)
    lf_rows = jnp.pad(lf_rows, ((0, 0), (0, padded - total)))
    c = _cumsum_lanes(lf_rows).reshape(batch, N_HEADS, padded)
    chunk = 1024
    c_cache = jnp.transpose(c[:, :, :past].reshape(batch, N_HEADS, past // chunk, chunk), (0, 2, 1, 3))
    c_new = c[:, :, past:past + LANES]
    o = _sample_attn(q, cache_k, cache_v, kb, vb, c_cache, c_new, batch, t_new, past, chunk)
    y = _attn_outproj(x, gate, o, wo, tm=x.shape[0])
    return y, k, v, lf


def _conv_weights(g_norm, w_in, conv_w, w_out):
    w = D_MODEL
    wb, wc, wu, wg = (w_in[:, i * w:(i + 1) * w].astype(BF16) for i in range(4))
    return _row(g_norm), wb, wc, wu, wg, conv_w.astype(F32), w_out.astype(BF16)


def kernel(x_prompt, x_sample, cache_k_l0, cache_v_l0, cache_logf_l0, state_conv_l1, cache_k_l2, cache_v_l2, cache_logf_l2, state_conv_l3, norm_l0, w_in_l0, b_f_l0, qnorm_l0, knorm_l0, w_out_l0, norm_l1, w_in_l1, conv_w_l1, w_out_l1, norm_l2, w_in_l2, b_f_l2, qnorm_l2, knorm_l2, w_out_l2, norm_l3, w_in_l3, conv_w_l3, w_out_l3):
    batch, seq, d = x_prompt.shape
    dec_batch, dec_seq, _ = x_sample.shape
    caches = [(cache_k_l0, cache_v_l0, cache_logf_l0), (state_conv_l1,),
              (cache_k_l2, cache_v_l2, cache_logf_l2), (state_conv_l3,)]
    params = [(norm_l0, w_in_l0, b_f_l0, qnorm_l0, knorm_l0, w_out_l0),
              (norm_l1, w_in_l1, conv_w_l1, w_out_l1),
              (norm_l2, w_in_l2, b_f_l2, qnorm_l2, knorm_l2, w_out_l2),
              (norm_l3, w_in_l3, conv_w_l3, w_out_l3)]
    tm, tk = 512, 512
    yp = x_prompt.reshape(batch * seq, d)
    ys = x_sample.reshape(dec_batch * dec_seq, d)
    outs = []
    for i in range(4):
        if i % 2 == 0:
            yp, kp, vp, lfp = _attn_prompt_layer(yp, params[i], batch, seq, tm, tk)
            ys, ks, vs, lfs = _attn_sample_layer(ys, *caches[i], params[i], dec_batch, dec_seq)
            outs += [kp.reshape(batch, seq, N_HEADS, HEAD_DIM), vp.reshape(batch, seq, N_HEADS, HEAD_DIM),
                     lfp.reshape(batch, seq, N_HEADS),
                     ks.reshape(dec_batch, dec_seq, N_HEADS, HEAD_DIM),
                     vs.reshape(dec_batch, dec_seq, N_HEADS, HEAD_DIM),
                     lfs.reshape(dec_batch, dec_seq, N_HEADS)]
        else:
            cw = _conv_weights(*params[i])
            zero_hist = jnp.zeros((batch, CONV_K - 1, d), F32)
            yp, cp = _conv_layer(yp, zero_hist, *cw, batch=batch, seq=seq, tm=tm)
            ys, cs = _conv_layer(ys, caches[i][0].astype(F32), *cw, batch=dec_batch, seq=dec_seq, tm=dec_seq)
            outs += [cp, cs]
    return (yp.reshape(batch, seq, d), ys.reshape(dec_batch, dec_seq, d), *outs)
```

```python
import functools

import jax
import jax.numpy as jnp
from jax import lax
from jax.experimental import pallas as pl
from jax.experimental.pallas import tpu as pltpu

N_HEADS = 8
HEAD_DIM = 128
D_MODEL = 1024
CONV_K = 3
NORM_EPS = 1e-6
LOG2E = 1.4426950408889634
QK_SCALE_LOG2 = HEAD_DIM ** -0.5 * LOG2E
NEG_BIG = -1e30
FAST_LOGIT_BOUND = 60.0

LANES = 128
V7X_SCOPED_VMEM_BYTES = 60000 * 1024

F32 = jnp.float32
BF16 = jnp.bfloat16


def _params(semantics, vmem_bytes):
    return pltpu.CompilerParams(dimension_semantics=semantics,
                                vmem_limit_bytes=min(int(vmem_bytes), V7X_SCOPED_VMEM_BYTES))


def _rms_scale(x):
    return lax.rsqrt(jnp.mean(x * x, axis=-1, keepdims=True) + NORM_EPS)


def _dot(a, b):
    return jnp.dot(a, b, preferred_element_type=F32)


def _dot_nt(a, b):
    return lax.dot_general(a, b, (((1,), (1,)), ((), ())), preferred_element_type=F32)


def _silu(g):
    return g / (1.0 + jnp.exp(-g))


def _attn_inproj_kernel(x_ref, gn_ref, wq_ref, wk_ref, wv_ref, wg_ref, wf_ref, bf_ref, gq_ref, gk_ref,
                        q_ref, k_ref, kb_ref, v_ref, vb_ref, g_ref, lf_ref):
    x = x_ref[...]
    hb = (x * _rms_scale(x) * gn_ref[...]).astype(BF16)

    tm = x.shape[0]
    zq = _dot(hb, wq_ref[...])
    zk = _dot(hb, wk_ref[...])
    zv = _dot(hb, wv_ref[...])
    gq = gq_ref[...] * QK_SCALE_LOG2
    gk = gk_ref[...]
    for h in range(N_HEADS):
        sl = slice(h * HEAD_DIM, (h + 1) * HEAD_DIM)
        rows = pl.ds(h, tm, stride=N_HEADS)
        qh = zq[:, sl]
        q_ref[:, sl] = (qh * _rms_scale(qh) * gq).astype(BF16)
        kh = zk[:, sl]
        kn = kh * _rms_scale(kh) * gk
        k_ref[rows, :] = kn
        kb_ref[:, sl] = kn.astype(BF16)
        v_ref[rows, :] = zv[:, sl]

    vb_ref[...] = zv.astype(BF16)
    g_ref[...] = _dot(hb, wg_ref[...]).astype(BF16)

    zf = _dot(hb, wf_ref[...]) + bf_ref[...]
    lf = jnp.minimum(zf, 0.0) - jnp.log1p(jnp.exp(-jnp.abs(zf)))
    lf_ref[...] = lf[:, :N_HEADS]


def _attn_inproj(x, gn, wq, wk, wv, wg, wf, bf, gq, gk, tm):
    n = x.shape[0]
    assert n % tm == 0
    row = lambda c: pl.BlockSpec((tm, c), lambda i: (i, 0))
    head_rows = pl.BlockSpec((tm * N_HEADS, HEAD_DIM), lambda i: (i, 0))
    full = lambda a: pl.BlockSpec(a.shape, lambda i: (0, 0))
    w_bytes = 2 * 2 * (4 * D_MODEL * D_MODEL + D_MODEL * LANES)
    io_bytes = 2 * tm * D_MODEL * (4 + 2 + 4 + 2 + 4 + 2 + 2)
    tmp_bytes = 6 * tm * D_MODEL * 4
    return pl.pallas_call(
        _attn_inproj_kernel,
        grid=(n // tm,),
        in_specs=[row(D_MODEL), full(gn), full(wq), full(wk), full(wv), full(wg), full(wf), full(bf),
                  full(gq), full(gk)],
        out_specs=[row(D_MODEL), head_rows, row(D_MODEL), head_rows, row(D_MODEL), row(D_MODEL),
                   row(N_HEADS)],
        out_shape=[jax.ShapeDtypeStruct((n, D_MODEL), BF16),
                   jax.ShapeDtypeStruct((n * N_HEADS, HEAD_DIM), F32),
                   jax.ShapeDtypeStruct((n, D_MODEL), BF16),
                   jax.ShapeDtypeStruct((n * N_HEADS, HEAD_DIM), F32),
                   jax.ShapeDtypeStruct((n, D_MODEL), BF16),
                   jax.ShapeDtypeStruct((n, D_MODEL), BF16),
                   jax.ShapeDtypeStruct((n, N_HEADS), F32)],
        compiler_params=_params(("parallel",), w_bytes + io_bytes + tmp_bytes),
        name="attn_inproj",
    )(x, gn, wq, wk, wv, wg, wf, bf, gq, gk)


def _cumsum_kernel(x_ref, u_ref, o_ref):
    rows, length = x_ref.shape
    u = u_ref[...]
    carry = jnp.zeros((rows, 1), F32)
    for j in range(length // LANES):
        sl = slice(j * LANES, (j + 1) * LANES)
        x = x_ref[:, sl]
        hi = x.astype(BF16)
        r1 = x - hi.astype(F32)
        mid = r1.astype(BF16)
        lo = (r1 - mid.astype(F32)).astype(BF16)
        c = (_dot(hi, u) + _dot(mid, u)) + _dot(lo, u) + carry
        o_ref[:, sl] = c * LOG2E
        carry = c[:, LANES - 1:LANES]


def _cumsum_lanes(x):
    rows, length = x.shape
    assert length % LANES == 0 and rows % 8 == 0
    idx = jnp.arange(LANES)
    u = (idx[:, None] <= idx[None, :]).astype(BF16)
    return pl.pallas_call(
        _cumsum_kernel,
        out_shape=jax.ShapeDtypeStruct((rows, length), F32),
        compiler_params=_params((), 8 * rows * length * 4 + (1 << 22)),
        name="cumsum_lanes",
    )(x, u)


def _causal_mask(blk):
    r = lax.broadcasted_iota(jnp.int32, (blk, blk), 0)
    c = lax.broadcasted_iota(jnp.int32, (blk, blk), 1)
    return c <= r


def _flash_online_kernel(q_ref, k_ref, v_ref, c_ref, o_ref, *, blk):
    i = pl.program_id(2)
    q = q_ref[...]

    def step(j, carry, masked):
        m, l, acc = carry
        off = pl.multiple_of(j * blk, blk)
        kb = k_ref[pl.ds(off, blk), :]
        vb = v_ref[pl.ds(off, blk), :]
        s = _dot_nt(q, kb) - c_ref[j]
        if masked:
            s = jnp.where(_causal_mask(blk), s, NEG_BIG)
        m_new = jnp.maximum(m, jnp.max(s, axis=-1, keepdims=True))
        p = jnp.exp2(s - m_new)
        alpha = jnp.exp2(m - m_new)
        l = alpha * l + jnp.sum(p, axis=-1, keepdims=True)
        acc = alpha * acc + _dot(p.astype(BF16), vb)
        return m_new, l, acc

    init = (jnp.full((blk, 1), NEG_BIG, F32), jnp.zeros((blk, 1), F32), jnp.zeros((blk, HEAD_DIM), F32))
    carry = lax.fori_loop(0, i, lambda j, c: step(j, c, False), init)
    m, l, acc = step(i, carry, True)
    o_ref[...] = (acc / l).astype(o_ref.dtype)


VT_ROWS = HEAD_DIM + 16


def _flash_bounded_kernel(q_ref, k_ref, v_ref, c_ref, o_ref, vt_ref, cs_ref, mask_ref, p_ref, acc_ref, *, tk):
    i = pl.program_id(2)
    tq = 2 * tk
    lane_tiles = tk // LANES

    @pl.when(i == 0)
    def _():
        sub = lax.broadcasted_iota(jnp.int32, (VT_ROWS - HEAD_DIM, tk), 0)
        ones_rows = jnp.where(sub == 0, 1.0, 0.0).astype(BF16)
        for jb in range(v_ref.shape[0] // tk):
            vt_ref[jb, :HEAD_DIM, :] = jnp.transpose(v_ref[jb * tk:(jb + 1) * tk, :].astype(F32)).astype(BF16)
            vt_ref[jb, HEAD_DIM:, :] = ones_rows
            c_row = c_ref[jb]
            for n in range(lane_tiles):
                lanes = slice(n * LANES, (n + 1) * LANES)
                cs_ref[jb, lanes, :] = jnp.transpose(jnp.broadcast_to(c_row[:, lanes], (LANES, LANES)))
        key = lax.broadcasted_iota(jnp.int32, (tk, tq), 0)
        query = lax.broadcasted_iota(jnp.int32, (tk, tq), 1)
        mask_ref[0] = jnp.zeros((tk, tq), F32)
        mask_ref[1] = jnp.where(key <= query, 0.0, NEG_BIG)

    c_t = jnp.concatenate([c_ref[2 * i], c_ref[2 * i + 1]], axis=1)

    def keys(j):
        return k_ref[pl.ds(pl.multiple_of(j * tk, tk), tk), :]

    def logits_t(j, cols):
        c_s = jnp.concatenate([cs_ref[j]] * lane_tiles, axis=1)
        return (_dot_nt(keys(j), q_ref[cols, :]) + c_t[:, cols]) - c_s

    def probs_t(slot, j, mask_index):
        for half in range(2):
            cols = slice(half * tk, (half + 1) * tk)
            z = logits_t(j, cols)
            if mask_index is not None:
                z = z + mask_ref[mask_index, :, cols]
            p_ref[slot, :, cols] = jnp.exp2(z).astype(BF16)

    def accumulate(slot, j):
        acc_ref[...] += _dot(vt_ref[j], p_ref[slot])

    first = 2 * i
    probs_t(0, 0, (i == 0).astype(jnp.int32))
    acc_ref[...] = jnp.zeros_like(acc_ref)

    def two_blocks(j):
        accumulate(0, j)
        probs_t(1, j + 1, None)
        accumulate(1, j + 1)
        probs_t(0, j + 2, (j + 2 == first).astype(jnp.int32))

    def trip(t, _):
        two_blocks(4 * t)
        two_blocks(4 * t + 2)
        return 0

    lax.fori_loop(0, i // 2, trip, 0)

    @pl.when(i % 2 == 1)
    def _():
        two_blocks(first - 2)

    accumulate(0, first)
    z = logits_t(first + 1, slice(tk, tq)) + mask_ref[1, :, :tk]
    acc_ref[:, tk:] += _dot(vt_ref[first + 1], jnp.exp2(z).astype(BF16))
    acc = acc_ref[...]
    o_ref[...] = jnp.transpose(acc[:HEAD_DIM, :] / acc[HEAD_DIM:HEAD_DIM + 1, :]).astype(o_ref.dtype)


def _flash_prompt(q, kb, vb, c, batch, seq, tk, bounded):
    tq = 2 * tk if bounded else tk
    nq = seq // tq
    assert seq % tq == 0
    qspec = pl.BlockSpec((tq, HEAD_DIM), lambda b, h, i: (b * nq + i, h))
    kvspec = pl.BlockSpec((seq, HEAD_DIM), lambda b, h, i: (b, h))
    cspec = pl.BlockSpec((None, seq // tk, 1, tk), lambda b, h, i: (b * N_HEADS + h, 0, 0, 0))
    vmem = 2 * 2 * seq * HEAD_DIM * 2 + 4 * tq * HEAD_DIM * 2 + 2 * seq * 4 + 8 * tq * tk * 4
    if bounded:
        body = functools.partial(_flash_bounded_kernel, tk=tk)
        scratch = [pltpu.VMEM((seq // tk, VT_ROWS, tk), BF16),
                   pltpu.VMEM((seq // tk, tk, LANES), F32),
                   pltpu.VMEM((2, tk, tq), F32),
                   pltpu.VMEM((2, tk, tq), BF16),
                   pltpu.VMEM((VT_ROWS, tq), F32)]
        vmem += seq * VT_ROWS * 2 + seq * LANES * 4 + 2 * tq * tk * 4 + 2 * tq * tk * 2 + tq * VT_ROWS * 4
    else:
        body = functools.partial(_flash_online_kernel, blk=tk)
        scratch = []
    return pl.pallas_call(
        body,
        grid=(batch, N_HEADS, nq),
        in_specs=[qspec, kvspec, kvspec, cspec],
        out_specs=qspec,
        out_shape=jax.ShapeDtypeStruct(q.shape, BF16),
        scratch_shapes=scratch,
        compiler_params=_params(("parallel", "parallel", "arbitrary"), vmem),
        name="flash_bounded" if bounded else "flash_online",
    )(q, kb, vb, c)


def _sample_attn_kernel(q_ref, ck_ref, cv_ref, kn_ref, vn_ref, cc_ref, cn_ref, o_ref, m_ref, l_ref, acc_ref,
                        *, chunk, t_new):
    step = pl.program_id(1)

    @pl.when(step == 0)
    def _():
        m_ref[...] = jnp.full_like(m_ref, NEG_BIG)
        l_ref[...] = jnp.zeros_like(l_ref)
        acc_ref[...] = jnp.zeros_like(acc_ref)

    def rescale(h, s):
        m_old = m_ref[h]
        m_new = jnp.maximum(m_old, jnp.max(s, axis=-1, keepdims=True))
        p = jnp.exp2(s - m_new)
        alpha = jnp.exp2(m_old - m_new)
        l_ref[h] = alpha * l_ref[h] + jnp.sum(p, axis=-1, keepdims=True)
        m_ref[h] = m_new
        return alpha, p.astype(BF16)

    def update(h, s, values):
        alpha, p = rescale(h, s)
        acc_ref[h] = alpha * acc_ref[h] + _dot(p, values)

    heads = range(N_HEADS)
    head_rows = [pl.ds(h, chunk, stride=N_HEADS) for h in heads]
    scores = [_dot_nt(q_ref[:, h * HEAD_DIM:(h + 1) * HEAD_DIM], ck_ref[head_rows[h], :].astype(BF16))
              - cc_ref[h:h + 1, :] for h in heads]
    scaled = [rescale(h, scores[h]) for h in heads]
    for h in heads:
        alpha, p = scaled[h]
        acc_ref[h] = alpha * acc_ref[h] + _dot(p, cv_ref[head_rows[h], :].astype(BF16))

    @pl.when(step == pl.num_programs(1) - 1)
    def _():
        r = lax.broadcasted_iota(jnp.int32, (t_new, t_new), 0)
        col = lax.broadcasted_iota(jnp.int32, (t_new, t_new), 1)
        for h in range(N_HEADS):
            sl = slice(h * HEAD_DIM, (h + 1) * HEAD_DIM)
            s = _dot_nt(q_ref[:, sl], kn_ref[:, sl]) - cn_ref[h:h + 1, :t_new]
            update(h, jnp.where(col <= r, s, NEG_BIG), vn_ref[:, sl])
            o_ref[:, sl] = (acc_ref[h] / l_ref[h]).astype(o_ref.dtype)


def _sample_attn(q, cache_k, cache_v, kb, vb, c_cache, c_new, batch, t_new, past, chunk):
    assert past % chunk == 0
    nspec = pl.BlockSpec((t_new, D_MODEL), lambda b, s: (b, 0))
    cachespec = pl.BlockSpec((None, chunk * N_HEADS, HEAD_DIM), lambda b, s: (b, s, 0))
    ccspec = pl.BlockSpec((None, None, N_HEADS, chunk), lambda b, s: (b, s, 0, 0))
    cnspec = pl.BlockSpec((None, N_HEADS, LANES), lambda b, s: (b, 0, 0))
    head_rows = (batch, past * N_HEADS, HEAD_DIM)
    vmem = 2 * 2 * chunk * N_HEADS * HEAD_DIM * 4 + 8 * chunk * HEAD_DIM * 4 + (1 << 22)
    return pl.pallas_call(
        functools.partial(_sample_attn_kernel, chunk=chunk, t_new=t_new),
        grid=(batch, past // chunk),
        in_specs=[nspec, cachespec, cachespec, nspec, nspec, ccspec, cnspec],
        out_specs=nspec,
        out_shape=jax.ShapeDtypeStruct(q.shape, BF16),
        scratch_shapes=[pltpu.VMEM((N_HEADS, t_new, 1), F32), pltpu.VMEM((N_HEADS, t_new, 1), F32),
                        pltpu.VMEM((N_HEADS, t_new, HEAD_DIM), F32)],
        compiler_params=_params(("parallel", "arbitrary"), vmem),
        name="sample_attn",
    )(q, cache_k.reshape(head_rows), cache_v.reshape(head_rows), kb, vb, c_cache, c_new)


def _outproj_conv_kernel(x_ref, g_ref, o_ref, wa_ref, hist_ref, gn_ref, wb_ref, wc_ref, wu_ref, wg_ref, cw_ref,
                         wo_ref, y_ref, st_ref, tail_ref, *, tm):
    @pl.when(pl.program_id(1) == 0)
    def _():
        tail_ref[...] = jnp.zeros_like(tail_ref)
        tail_ref[8 - (CONV_K - 1):, :] = hist_ref[...]

    gated = _silu(g_ref[...].astype(F32)) * o_ref[...].astype(F32)
    x = x_ref[...] + _dot(gated.astype(BF16), wa_ref[...])
    hb = (x * _rms_scale(x) * gn_ref[...]).astype(BF16)
    cu = _dot(hb, wc_ref[...]) * _dot(hb, wu_ref[...])

    tail = tail_ref[...]
    h_m2 = tail[6:7, :]
    h_m1 = tail[7:8, :]
    r = lax.broadcasted_iota(jnp.int32, cu.shape, 0)
    s1 = jnp.where(r == 0, h_m1, pltpu.roll(cu, 1, 0))
    s2 = jnp.where(r == 0, h_m2, jnp.where(r == 1, h_m1, pltpu.roll(cu, 2, 0)))
    cw = cw_ref[...]
    conv = cw[0:1, :] * s2 + cw[1:2, :] * s1 + cw[2:3, :] * cu

    tail_ref[...] = cu[tm - 8:, :]
    st_ref[...] = cu[tm - (CONV_K - 1):, :]

    mixed = _silu(_dot(hb, wg_ref[...])) * _dot(hb, wb_ref[...]) * conv
    y_ref[...] = x + _dot(mixed.astype(BF16), wo_ref[...])


def _outproj_conv(x, gate, o, wa, hist, gn, wb, wc, wu, wg, cw, wo, batch, seq, tm):
    assert seq % tm == 0 and tm % 8 == 0 and tm >= 8
    nt = seq // tm
    row = pl.BlockSpec((tm, D_MODEL), lambda b, t: (b * nt + t, 0))
    st = pl.BlockSpec((None, CONV_K - 1, D_MODEL), lambda b, t: (b, 0, 0))
    full = lambda a: pl.BlockSpec(a.shape, lambda b, t: (0, 0))
    w_bytes = 2 * 2 * 6 * D_MODEL * D_MODEL
    io_bytes = 2 * tm * D_MODEL * (4 + 2 + 2 + 4)
    tmp_bytes = 10 * tm * D_MODEL * 4
    return pl.pallas_call(
        functools.partial(_outproj_conv_kernel, tm=tm),
        grid=(batch, nt),
        in_specs=[row, row, row, full(wa), st, full(gn), full(wb), full(wc), full(wu), full(wg), full(cw),
                  full(wo)],
        out_specs=[row, st],
        out_shape=[jax.ShapeDtypeStruct(x.shape, F32),
                   jax.ShapeDtypeStruct((batch, CONV_K - 1, D_MODEL), F32)],
        scratch_shapes=[pltpu.VMEM((8, D_MODEL), F32)],
        compiler_params=_params(("parallel", "arbitrary"), w_bytes + io_bytes + tmp_bytes),
        name="outproj_conv",
    )(x, gate, o, wa, hist, gn, wb, wc, wu, wg, cw, wo)


def _row(v, width=None):
    v = v.astype(F32).reshape(1, -1)
    if width is not None and v.shape[1] < width:
        v = jnp.pad(v, ((0, 0), (0, width - v.shape[1])))
    return v


def _attn_weights(g_norm, w_in, b_f, g_q, g_k, w_out):
    aw = N_HEADS * HEAD_DIM
    wq, wk, wv, wg = (w_in[:, i * aw:(i + 1) * aw].astype(BF16) for i in range(4))
    wf = jnp.pad(w_in[:, 4 * aw:], ((0, 0), (0, LANES - N_HEADS))).astype(BF16)
    return (_row(g_norm), wq, wk, wv, wg, wf, _row(b_f, LANES), _row(g_q), _row(g_k)), w_out.astype(BF16)


def _heads_to_rows(lf, batch, length):
    return jnp.transpose(lf.reshape(batch, length, N_HEADS), (0, 2, 1)).reshape(batch * N_HEADS, length)


def _attn_prompt_layer(x, params, batch, seq, tm, tk):
    inw, wo = _attn_weights(*params)
    q, k, kb, v, vb, gate, lf = _attn_inproj(x, *inw, tm=tm)
    c = _cumsum_lanes(_heads_to_rows(lf, batch, seq))
    c = c.reshape(batch * N_HEADS, seq // tk, 1, tk)
    g_q, g_k = params[3], params[4]
    logit_bound = 1.01 * HEAD_DIM ** 0.5 * jnp.max(jnp.abs(g_q)) * jnp.max(jnp.abs(g_k))
    o = lax.cond(logit_bound < FAST_LOGIT_BOUND,
                 functools.partial(_flash_prompt, batch=batch, seq=seq, tk=tk, bounded=True),
                 functools.partial(_flash_prompt, batch=batch, seq=seq, tk=tk, bounded=False),
                 q, kb, vb, c)
    return (gate, o, wo), k, v, lf


def _attn_sample_layer(x, cache_k, cache_v, cache_logf, params, batch, t_new):
    past = cache_k.shape[1]
    inw, wo = _attn_weights(*params)
    q, k, kb, v, vb, gate, lf = _attn_inproj(x, *inw, tm=x.shape[0])
    lf_all = jnp.concatenate([cache_logf.astype(F32), lf.reshape(batch, t_new, N_HEADS)], axis=1)
    total = past + t_new
    padded = -(-total // LANES) * LANES
    lf_rows = _heads_to_rows(lf_all.reshape(batch * total, N_HEADS), batch, total)
    lf_rows = jnp.pad(lf_rows, ((0, 0), (0, padded - total)))
    c = _cumsum_lanes(lf_rows).reshape(batch, N_HEADS, padded)
    chunk = 1024
    c_cache = jnp.transpose(c[:, :, :past].reshape(batch, N_HEADS, past // chunk, chunk), (0, 2, 1, 3))
    c_new = c[:, :, past:past + LANES]
    o = _sample_attn(q, cache_k, cache_v, kb, vb, c_cache, c_new, batch, t_new, past, chunk)
    return (gate, o, wo), k, v, lf


def _conv_weights(g_norm, w_in, conv_w, w_out):
    w = D_MODEL
    wb, wc, wu, wg = (w_in[:, i * w:(i + 1) * w].astype(BF16) for i in range(4))
    return _row(g_norm), wb, wc, wu, wg, conv_w.astype(F32), w_out.astype(BF16)


def kernel(x_prompt, x_sample, cache_k_l0, cache_v_l0, cache_logf_l0, state_conv_l1, cache_k_l2, cache_v_l2, cache_logf_l2, state_conv_l3, norm_l0, w_in_l0, b_f_l0, qnorm_l0, knorm_l0, w_out_l0, norm_l1, w_in_l1, conv_w_l1, w_out_l1, norm_l2, w_in_l2, b_f_l2, qnorm_l2, knorm_l2, w_out_l2, norm_l3, w_in_l3, conv_w_l3, w_out_l3):
    batch, seq, d = x_prompt.shape
    dec_batch, dec_seq, _ = x_sample.shape
    caches = [(cache_k_l0, cache_v_l0, cache_logf_l0), (state_conv_l1,),
              (cache_k_l2, cache_v_l2, cache_logf_l2), (state_conv_l3,)]
    params = [(norm_l0, w_in_l0, b_f_l0, qnorm_l0, knorm_l0, w_out_l0),
              (norm_l1, w_in_l1, conv_w_l1, w_out_l1),
              (norm_l2, w_in_l2, b_f_l2, qnorm_l2, knorm_l2, w_out_l2),
              (norm_l3, w_in_l3, conv_w_l3, w_out_l3)]
    tm, tk = 512, 512
    yp = x_prompt.reshape(batch * seq, d)
    ys = x_sample.reshape(dec_batch * dec_seq, d)
    outs = []
    for i in range(0, len(params), 2):
        mix_p, kp, vp, lfp = _attn_prompt_layer(yp, params[i], batch, seq, tm, tk)
        mix_s, ks, vs, lfs = _attn_sample_layer(ys, *caches[i], params[i], dec_batch, dec_seq)
        outs += [kp.reshape(batch, seq, N_HEADS, HEAD_DIM), vp.reshape(batch, seq, N_HEADS, HEAD_DIM),
                 lfp.reshape(batch, seq, N_HEADS),
                 ks.reshape(dec_batch, dec_seq, N_HEADS, HEAD_DIM),
                 vs.reshape(dec_batch, dec_seq, N_HEADS, HEAD_DIM),
                 lfs.reshape(dec_batch, dec_seq, N_HEADS)]
        cw = _conv_weights(*params[i + 1])
        zero_hist = jnp.zeros((batch, CONV_K - 1, d), F32)
        yp, cp = _outproj_conv(yp, *mix_p, zero_hist, *cw, batch=batch, seq=seq, tm=tm)
        ys, cs = _outproj_conv(ys, *mix_s, caches[i + 1][0].astype(F32), *cw, batch=dec_batch, seq=dec_seq,
                               tm=dec_seq)
        outs += [cp, cs]
    return (yp.reshape(batch, seq, d), ys.reshape(dec_batch, dec_seq, d), *outs)
```

```python
import functools

import jax
import jax.numpy as jnp
from jax import lax
from jax.experimental import pallas as pl
from jax.experimental.pallas import tpu as pltpu

N_HEADS = 8
HEAD_DIM = 128
D_MODEL = 1024
CONV_K = 3
NORM_EPS = 1e-6
LOG2E = 1.4426950408889634
QK_SCALE_LOG2 = HEAD_DIM ** -0.5 * LOG2E
NEG_BIG = -1e30
FAST_LOGIT_BOUND = 60.0

LANES = 128
V7X_SCOPED_VMEM_BYTES = 60000 * 1024

F32 = jnp.float32
BF16 = jnp.bfloat16


def _params(semantics, vmem_bytes):
    return pltpu.CompilerParams(dimension_semantics=semantics,
                                vmem_limit_bytes=min(int(vmem_bytes), V7X_SCOPED_VMEM_BYTES))


def _rms_scale(x):
    return lax.rsqrt(jnp.mean(x * x, axis=-1, keepdims=True) + NORM_EPS)


def _dot(a, b):
    return jnp.dot(a, b, preferred_element_type=F32)


def _dot_nt(a, b):
    return lax.dot_general(a, b, (((1,), (1,)), ((), ())), preferred_element_type=F32)


def _silu(g):
    return g / (1.0 + jnp.exp(-g))


def _attn_inproj_kernel(x_ref, gn_ref, wq_ref, wk_ref, wv_ref, wg_ref, wf_ref, bf_ref, gq_ref, gk_ref,
                        q_ref, k_ref, kb_ref, v_ref, vb_ref, g_ref, lf_ref):
    x = x_ref[...]
    hb = (x * _rms_scale(x) * gn_ref[...]).astype(BF16)

    tm = x.shape[0]
    zq = _dot(hb, wq_ref[...])
    zk = _dot(hb, wk_ref[...])
    zv = _dot(hb, wv_ref[...])
    gq = gq_ref[...] * QK_SCALE_LOG2
    gk = gk_ref[...]
    for h in range(N_HEADS):
        sl = slice(h * HEAD_DIM, (h + 1) * HEAD_DIM)
        rows = pl.ds(h, tm, stride=N_HEADS)
        qh = zq[:, sl]
        q_ref[:, sl] = (qh * _rms_scale(qh) * gq).astype(BF16)
        kh = zk[:, sl]
        kn = kh * _rms_scale(kh) * gk
        k_ref[rows, :] = kn
        kb_ref[:, sl] = kn.astype(BF16)
        v_ref[rows, :] = zv[:, sl]

    vb_ref[...] = zv.astype(BF16)
    g_ref[...] = _dot(hb, wg_ref[...]).astype(BF16)

    zf = _dot(hb, wf_ref[...]) + bf_ref[...]
    lf = jnp.minimum(zf, 0.0) - jnp.log1p(jnp.exp(-jnp.abs(zf)))
    lf_ref[...] = lf[:, :N_HEADS]


def _attn_inproj(x, gn, wq, wk, wv, wg, wf, bf, gq, gk, tm):
    n = x.shape[0]
    assert n % tm == 0
    row = lambda c: pl.BlockSpec((tm, c), lambda i: (i, 0))
    head_rows = pl.BlockSpec((tm * N_HEADS, HEAD_DIM), lambda i: (i, 0))
    full = lambda a: pl.BlockSpec(a.shape, lambda i: (0, 0))
    w_bytes = 2 * 2 * (4 * D_MODEL * D_MODEL + D_MODEL * LANES)
    io_bytes = 2 * tm * D_MODEL * (4 + 2 + 4 + 2 + 4 + 2 + 2)
    tmp_bytes = 6 * tm * D_MODEL * 4
    return pl.pallas_call(
        _attn_inproj_kernel,
        grid=(n // tm,),
        in_specs=[row(D_MODEL), full(gn), full(wq), full(wk), full(wv), full(wg), full(wf), full(bf),
                  full(gq), full(gk)],
        out_specs=[row(D_MODEL), head_rows, row(D_MODEL), head_rows, row(D_MODEL), row(D_MODEL),
                   row(N_HEADS)],
        out_shape=[jax.ShapeDtypeStruct((n, D_MODEL), BF16),
                   jax.ShapeDtypeStruct((n * N_HEADS, HEAD_DIM), F32),
                   jax.ShapeDtypeStruct((n, D_MODEL), BF16),
                   jax.ShapeDtypeStruct((n * N_HEADS, HEAD_DIM), F32),
                   jax.ShapeDtypeStruct((n, D_MODEL), BF16),
                   jax.ShapeDtypeStruct((n, D_MODEL), BF16),
                   jax.ShapeDtypeStruct((n, N_HEADS), F32)],
        compiler_params=_params(("parallel",), w_bytes + io_bytes + tmp_bytes),
        name="attn_inproj",
    )(x, gn, wq, wk, wv, wg, wf, bf, gq, gk)


def _cumsum_kernel(x_ref, u_ref, o_ref):
    rows, length = x_ref.shape
    u = u_ref[...]
    carry = jnp.zeros((rows, 1), F32)
    for j in range(length // LANES):
        sl = slice(j * LANES, (j + 1) * LANES)
        x = x_ref[:, sl]
        hi = x.astype(BF16)
        r1 = x - hi.astype(F32)
        mid = r1.astype(BF16)
        lo = (r1 - mid.astype(F32)).astype(BF16)
        c = (_dot(hi, u) + _dot(mid, u)) + _dot(lo, u) + carry
        o_ref[:, sl] = c * LOG2E
        carry = c[:, LANES - 1:LANES]


def _cumsum_lanes(x):
    rows, length = x.shape
    assert length % LANES == 0 and rows % 8 == 0
    idx = jnp.arange(LANES)
    u = (idx[:, None] <= idx[None, :]).astype(BF16)
    return pl.pallas_call(
        _cumsum_kernel,
        out_shape=jax.ShapeDtypeStruct((rows, length), F32),
        compiler_params=_params((), 8 * rows * length * 4 + (1 << 22)),
        name="cumsum_lanes",
    )(x, u)


def _causal_mask(blk):
    r = lax.broadcasted_iota(jnp.int32, (blk, blk), 0)
    c = lax.broadcasted_iota(jnp.int32, (blk, blk), 1)
    return c <= r


def _flash_online_kernel(q_ref, k_ref, v_ref, c_ref, o_ref, *, blk):
    i = pl.program_id(2)
    q = q_ref[...]

    def step(j, carry, masked):
        m, l, acc = carry
        off = pl.multiple_of(j * blk, blk)
        kb = k_ref[pl.ds(off, blk), :]
        vb = v_ref[pl.ds(off, blk), :]
        s = _dot_nt(q, kb) - c_ref[j]
        if masked:
            s = jnp.where(_causal_mask(blk), s, NEG_BIG)
        m_new = jnp.maximum(m, jnp.max(s, axis=-1, keepdims=True))
        p = jnp.exp2(s - m_new)
        alpha = jnp.exp2(m - m_new)
        l = alpha * l + jnp.sum(p, axis=-1, keepdims=True)
        acc = alpha * acc + _dot(p.astype(BF16), vb)
        return m_new, l, acc

    init = (jnp.full((blk, 1), NEG_BIG, F32), jnp.zeros((blk, 1), F32), jnp.zeros((blk, HEAD_DIM), F32))
    carry = lax.fori_loop(0, i, lambda j, c: step(j, c, False), init)
    m, l, acc = step(i, carry, True)
    o_ref[...] = (acc / l).astype(o_ref.dtype)


VT_ROWS = HEAD_DIM + 16
FLASH_KEY_BLOCKS_PER_QUERY_BLOCK = 8


def _flash_bounded_kernel(q_ref, k_ref, v_ref, c_ref, o_ref, vt_ref, cs_ref, mask_ref, p_ref, acc_ref, *, tk, r):
    i = pl.program_id(2)
    tq = r * tk
    lane_tiles = tk // LANES

    @pl.when(i == 0)
    def _():
        sub = lax.broadcasted_iota(jnp.int32, (VT_ROWS - HEAD_DIM, tk), 0)
        ones_rows = jnp.where(sub == 0, 1.0, 0.0).astype(BF16)
        for jb in range(v_ref.shape[0] // tk):
            vt_ref[jb, :HEAD_DIM, :] = jnp.transpose(v_ref[jb * tk:(jb + 1) * tk, :].astype(F32)).astype(BF16)
            vt_ref[jb, HEAD_DIM:, :] = ones_rows
            c_row = c_ref[jb]
            for n in range(lane_tiles):
                lanes = slice(n * LANES, (n + 1) * LANES)
                cs_ref[jb, lanes, :] = jnp.transpose(jnp.broadcast_to(c_row[:, lanes], (LANES, LANES)))
        key = lax.broadcasted_iota(jnp.int32, (tk, tk), 0)
        query = lax.broadcasted_iota(jnp.int32, (tk, tk), 1)
        mask_ref[0] = jnp.zeros((tk, tk), F32)
        mask_ref[1] = jnp.where(key <= query, 0.0, NEG_BIG)

    first = r * i
    c_t = jnp.concatenate([c_ref[first + a] for a in range(r)], axis=1)

    def keys(j):
        return k_ref[pl.ds(pl.multiple_of(j * tk, tk), tk), :]

    def logits_t(j, cols):
        c_s = jnp.concatenate([cs_ref[j]] * ((cols.stop - cols.start) // LANES), axis=1)
        return (_dot_nt(keys(j), q_ref[cols, :]) + c_t[:, cols]) - c_s

    def probs_t(slot, j, mask_index, first_part=0):
        for part in range(first_part, r):
            cols = slice(part * tk, (part + 1) * tk)
            z = logits_t(j, cols)
            if mask_index is not None and part == first_part:
                z = z + mask_ref[mask_index]
            p_ref[slot, :, cols] = jnp.exp2(z).astype(BF16)

    def accumulate(slot, j):
        for half in range(2):
            cols = slice(half * tq // 2, (half + 1) * tq // 2)
            acc_ref[:, cols] += _dot(vt_ref[j], p_ref[slot, :, cols])

    probs_t(0, 0, (i == 0).astype(jnp.int32))
    acc_ref[...] = jnp.zeros_like(acc_ref)

    def two_blocks(j):
        accumulate(0, j)
        probs_t(1, j + 1, None)
        accumulate(1, j + 1)
        probs_t(0, j + 2, (j + 2 == first).astype(jnp.int32))

    def trip(t, _):
        two_blocks(4 * t)
        two_blocks(4 * t + 2)
        return 0

    lax.fori_loop(0, first // 4, trip, 0)
    accumulate(0, first)
    for d in range(1, r):
        slot = d % 2
        probs_t(slot, first + d, 1, first_part=d)
        acc_ref[:, d * tk:] += _dot(vt_ref[first + d], p_ref[slot, :, d * tk:])
    acc = acc_ref[...]
    o_ref[...] = jnp.transpose(acc[:HEAD_DIM, :] / acc[HEAD_DIM:HEAD_DIM + 1, :]).astype(o_ref.dtype)


def _flash_prompt(q, kb, vb, c, batch, seq, tk, bounded):
    r = FLASH_KEY_BLOCKS_PER_QUERY_BLOCK if bounded else 1
    tq = r * tk
    nq = seq // tq
    assert seq % tq == 0 and (r == 1 or r % 4 == 0)
    qspec = pl.BlockSpec((tq, HEAD_DIM), lambda b, h, i: (b * nq + i, h))
    kvspec = pl.BlockSpec((seq, HEAD_DIM), lambda b, h, i: (b, h))
    cspec = pl.BlockSpec((None, seq // tk, 1, tk), lambda b, h, i: (b * N_HEADS + h, 0, 0, 0))
    vmem = 2 * 2 * seq * HEAD_DIM * 2 + 4 * tq * HEAD_DIM * 2 + 2 * seq * 4 + 8 * tq * tk * 4
    if bounded:
        body = functools.partial(_flash_bounded_kernel, tk=tk, r=r)
        scratch = [pltpu.VMEM((seq // tk, VT_ROWS, tk), BF16),
                   pltpu.VMEM((seq // tk, tk, LANES), F32),
                   pltpu.VMEM((2, tk, tk), F32),
                   pltpu.VMEM((2, tk, tq), BF16),
                   pltpu.VMEM((VT_ROWS, tq), F32)]
        vmem += seq * VT_ROWS * 2 + seq * LANES * 4 + 2 * tk * tk * 4 + 2 * tq * tk * 2 + tq * VT_ROWS * 4
    else:
        body = functools.partial(_flash_online_kernel, blk=tk)
        scratch = []
    return pl.pallas_call(
        body,
        grid=(batch, N_HEADS, nq),
        in_specs=[qspec, kvspec, kvspec, cspec],
        out_specs=qspec,
        out_shape=jax.ShapeDtypeStruct(q.shape, BF16),
        scratch_shapes=scratch,
        compiler_params=_params(("parallel", "parallel", "arbitrary"), vmem),
        name="flash_bounded" if bounded else "flash_online",
    )(q, kb, vb, c)


def _sample_attn_kernel(q_ref, ck_ref, cv_ref, kn_ref, vn_ref, cc_ref, cn_ref, o_ref, m_ref, l_ref, acc_ref,
                        *, chunk, t_new):
    step = pl.program_id(1)

    @pl.when(step == 0)
    def _():
        m_ref[...] = jnp.full_like(m_ref, NEG_BIG)
        l_ref[...] = jnp.zeros_like(l_ref)
        acc_ref[...] = jnp.zeros_like(acc_ref)

    def rescale(h, s):
        m_old = m_ref[h]
        m_new = jnp.maximum(m_old, jnp.max(s, axis=-1, keepdims=True))
        p = jnp.exp2(s - m_new)
        alpha = jnp.exp2(m_old - m_new)
        l_ref[h] = alpha * l_ref[h] + jnp.sum(p, axis=-1, keepdims=True)
        m_ref[h] = m_new
        return alpha, p.astype(BF16)

    def update(h, s, values):
        alpha, p = rescale(h, s)
        acc_ref[h] = alpha * acc_ref[h] + _dot(p, values)

    heads = range(N_HEADS)
    head_rows = [pl.ds(h, chunk, stride=N_HEADS) for h in heads]
    scores = [_dot_nt(q_ref[:, h * HEAD_DIM:(h + 1) * HEAD_DIM], ck_ref[head_rows[h], :].astype(BF16))
              - cc_ref[h:h + 1, :] for h in heads]
    scaled = [rescale(h, scores[h]) for h in heads]
    for h in heads:
        alpha, p = scaled[h]
        acc_ref[h] = alpha * acc_ref[h] + _dot(p, cv_ref[head_rows[h], :].astype(BF16))

    @pl.when(step == pl.num_programs(1) - 1)
    def _():
        r = lax.broadcasted_iota(jnp.int32, (t_new, t_new), 0)
        col = lax.broadcasted_iota(jnp.int32, (t_new, t_new), 1)
        for h in range(N_HEADS):
            sl = slice(h * HEAD_DIM, (h + 1) * HEAD_DIM)
            s = _dot_nt(q_ref[:, sl], kn_ref[:, sl]) - cn_ref[h:h + 1, :t_new]
            update(h, jnp.where(col <= r, s, NEG_BIG), vn_ref[:, sl])
            o_ref[:, sl] = (acc_ref[h] / l_ref[h]).astype(o_ref.dtype)


def _sample_attn(q, cache_k, cache_v, kb, vb, c_cache, c_new, batch, t_new, past, chunk):
    assert past % chunk == 0
    nspec = pl.BlockSpec((t_new, D_MODEL), lambda b, s: (b, 0))
    cachespec = pl.BlockSpec((None, chunk * N_HEADS, HEAD_DIM), lambda b, s: (b, s, 0))
    ccspec = pl.BlockSpec((None, None, N_HEADS, chunk), lambda b, s: (b, s, 0, 0))
    cnspec = pl.BlockSpec((None, N_HEADS, LANES), lambda b, s: (b, 0, 0))
    head_rows = (batch, past * N_HEADS, HEAD_DIM)
    vmem = 2 * 2 * chunk * N_HEADS * HEAD_DIM * 4 + 8 * chunk * HEAD_DIM * 4 + (1 << 22)
    return pl.pallas_call(
        functools.partial(_sample_attn_kernel, chunk=chunk, t_new=t_new),
        grid=(batch, past // chunk),
        in_specs=[nspec, cachespec, cachespec, nspec, nspec, ccspec, cnspec],
        out_specs=nspec,
        out_shape=jax.ShapeDtypeStruct(q.shape, BF16),
        scratch_shapes=[pltpu.VMEM((N_HEADS, t_new, 1), F32), pltpu.VMEM((N_HEADS, t_new, 1), F32),
                        pltpu.VMEM((N_HEADS, t_new, HEAD_DIM), F32)],
        compiler_params=_params(("parallel", "arbitrary"), vmem),
        name="sample_attn",
    )(q, cache_k.reshape(head_rows), cache_v.reshape(head_rows), kb, vb, c_cache, c_new)


def _outproj_conv_kernel(x_ref, g_ref, o_ref, wa_ref, hist_ref, gn_ref, wb_ref, wc_ref, wu_ref, wg_ref, cw_ref,
                         wo_ref, y_ref, st_ref, tail_ref, *, tm):
    @pl.when(pl.program_id(1) == 0)
    def _():
        tail_ref[...] = jnp.zeros_like(tail_ref)
        tail_ref[8 - (CONV_K - 1):, :] = hist_ref[...]

    gated = _silu(g_ref[...].astype(F32)) * o_ref[...].astype(F32)
    x = x_ref[...] + _dot(gated.astype(BF16), wa_ref[...])
    hb = (x * _rms_scale(x) * gn_ref[...]).astype(BF16)
    cu = _dot(hb, wc_ref[...]) * _dot(hb, wu_ref[...])

    tail = tail_ref[...]
    h_m2 = tail[6:7, :]
    h_m1 = tail[7:8, :]
    r = lax.broadcasted_iota(jnp.int32, cu.shape, 0)
    s1 = jnp.where(r == 0, h_m1, pltpu.roll(cu, 1, 0))
    s2 = jnp.where(r == 0, h_m2, jnp.where(r == 1, h_m1, pltpu.roll(cu, 2, 0)))
    cw = cw_ref[...]
    conv = cw[0:1, :] * s2 + cw[1:2, :] * s1 + cw[2:3, :] * cu

    tail_ref[...] = cu[tm - 8:, :]
    st_ref[...] = cu[tm - (CONV_K - 1):, :]

    mixed = _silu(_dot(hb, wg_ref[...])) * _dot(hb, wb_ref[...]) * conv
    y_ref[...] = x + _dot(mixed.astype(BF16), wo_ref[...])


def _outproj_conv(x, gate, o, wa, hist, gn, wb, wc, wu, wg, cw, wo, batch, seq, tm):
    assert seq % tm == 0 and tm % 8 == 0 and tm >= 8
    nt = seq // tm
    row = pl.BlockSpec((tm, D_MODEL), lambda b, t: (b * nt + t, 0))
    st = pl.BlockSpec((None, CONV_K - 1, D_MODEL), lambda b, t: (b, 0, 0))
    full = lambda a: pl.BlockSpec(a.shape, lambda b, t: (0, 0))
    w_bytes = 2 * 2 * 6 * D_MODEL * D_MODEL
    io_bytes = 2 * tm * D_MODEL * (4 + 2 + 2 + 4)
    tmp_bytes = 10 * tm * D_MODEL * 4
    return pl.pallas_call(
        functools.partial(_outproj_conv_kernel, tm=tm),
        grid=(batch, nt),
        in_specs=[row, row, row, full(wa), st, full(gn), full(wb), full(wc), full(wu), full(wg), full(cw),
                  full(wo)],
        out_specs=[row, st],
        out_shape=[jax.ShapeDtypeStruct(x.shape, F32),
                   jax.ShapeDtypeStruct((batch, CONV_K - 1, D_MODEL), F32)],
        scratch_shapes=[pltpu.VMEM((8, D_MODEL), F32)],
        compiler_params=_params(("parallel", "arbitrary"), w_bytes + io_bytes + tmp_bytes),
        name="outproj_conv",
    )(x, gate, o, wa, hist, gn, wb, wc, wu, wg, cw, wo)


def _row(v, width=None):
    v = v.astype(F32).reshape(1, -1)
    if width is not None and v.shape[1] < width:
        v = jnp.pad(v, ((0, 0), (0, width - v.shape[1])))
    return v


def _attn_weights(g_norm, w_in, b_f, g_q, g_k, w_out):
    aw = N_HEADS * HEAD_DIM
    wq, wk, wv, wg = (w_in[:, i * aw:(i + 1) * aw].astype(BF16) for i in range(4))
    wf = jnp.pad(w_in[:, 4 * aw:], ((0, 0), (0, LANES - N_HEADS))).astype(BF16)
    return (_row(g_norm), wq, wk, wv, wg, wf, _row(b_f, LANES), _row(g_q), _row(g_k)), w_out.astype(BF16)


def _heads_to_rows(lf, batch, length):
    return jnp.transpose(lf.reshape(batch, length, N_HEADS), (0, 2, 1)).reshape(batch * N_HEADS, length)


def _attn_prompt_layer(x, params, batch, seq, tm, tk):
    inw, wo = _attn_weights(*params)
    q, k, kb, v, vb, gate, lf = _attn_inproj(x, *inw, tm=tm)
    c = _cumsum_lanes(_heads_to_rows(lf, batch, seq))
    c = c.reshape(batch * N_HEADS, seq // tk, 1, tk)
    g_q, g_k = params[3], params[4]
    logit_bound = 1.01 * HEAD_DIM ** 0.5 * jnp.max(jnp.abs(g_q)) * jnp.max(jnp.abs(g_k))
    o = lax.cond(logit_bound < FAST_LOGIT_BOUND,
                 functools.partial(_flash_prompt, batch=batch, seq=seq, tk=tk, bounded=True),
                 functools.partial(_flash_prompt, batch=batch, seq=seq, tk=tk, bounded=False),
                 q, kb, vb, c)
    return (gate, o, wo), k, v, lf


def _attn_sample_layer(x, cache_k, cache_v, cache_logf, params, batch, t_new):
    past = cache_k.shape[1]
    inw, wo = _attn_weights(*params)
    q, k, kb, v, vb, gate, lf = _attn_inproj(x, *inw, tm=x.shape[0])
    lf_all = jnp.concatenate([cache_logf.astype(F32), lf.reshape(batch, t_new, N_HEADS)], axis=1)
    total = past + t_new
    padded = -(-total // LANES) * LANES
    lf_rows = _heads_to_rows(lf_all.reshape(batch * total, N_HEADS), batch, total)
    lf_rows = jnp.pad(lf_rows, ((0, 0), (0, padded - total)))
    c = _cumsum_lanes(lf_rows).reshape(batch, N_HEADS, padded)
    chunk = 1024
    c_cache = jnp.transpose(c[:, :, :past].reshape(batch, N_HEADS, past // chunk, chunk), (0, 2, 1, 3))
    c_new = c[:, :, past:past + LANES]
    o = _sample_attn(q, cache_k, cache_v, kb, vb, c_cache, c_new, batch, t_new, past, chunk)
    return (gate, o, wo), k, v, lf


def _conv_weights(g_norm, w_in, conv_w, w_out):
    w = D_MODEL
    wb, wc, wu, wg = (w_in[:, i * w:(i + 1) * w].astype(BF16) for i in range(4))
    return _row(g_norm), wb, wc, wu, wg, conv_w.astype(F32), w_out.astype(BF16)


def kernel(x_prompt, x_sample, cache_k_l0, cache_v_l0, cache_logf_l0, state_conv_l1, cache_k_l2, cache_v_l2, cache_logf_l2, state_conv_l3, norm_l0, w_in_l0, b_f_l0, qnorm_l0, knorm_l0, w_out_l0, norm_l1, w_in_l1, conv_w_l1, w_out_l1, norm_l2, w_in_l2, b_f_l2, qnorm_l2, knorm_l2, w_out_l2, norm_l3, w_in_l3, conv_w_l3, w_out_l3):
    batch, seq, d = x_prompt.shape
    dec_batch, dec_seq, _ = x_sample.shape
    caches = [(cache_k_l0, cache_v_l0, cache_logf_l0), (state_conv_l1,),
              (cache_k_l2, cache_v_l2, cache_logf_l2), (state_conv_l3,)]
    params = [(norm_l0, w_in_l0, b_f_l0, qnorm_l0, knorm_l0, w_out_l0),
              (norm_l1, w_in_l1, conv_w_l1, w_out_l1),
              (norm_l2, w_in_l2, b_f_l2, qnorm_l2, knorm_l2, w_out_l2),
              (norm_l3, w_in_l3, conv_w_l3, w_out_l3)]
    tm, tk = 512, 512
    yp = x_prompt.reshape(batch * seq, d)
    ys = x_sample.reshape(dec_batch * dec_seq, d)
    outs = []
    for i in range(0, len(params), 2):
        mix_p, kp, vp, lfp = _attn_prompt_layer(yp, params[i], batch, seq, tm, tk)
        mix_s, ks, vs, lfs = _attn_sample_layer(ys, *caches[i], params[i], dec_batch, dec_seq)
        outs += [kp.reshape(batch, seq, N_HEADS, HEAD_DIM), vp.reshape(batch, seq, N_HEADS, HEAD_DIM),
                 lfp.reshape(batch, seq, N_HEADS),
                 ks.reshape(dec_batch, dec_seq, N_HEADS, HEAD_DIM),
                 vs.reshape(dec_batch, dec_seq, N_HEADS, HEAD_DIM),
                 lfs.reshape(dec_batch, dec_seq, N_HEADS)]
        cw = _conv_weights(*params[i + 1])
        zero_hist = jnp.zeros((batch, CONV_K - 1, d), F32)
        yp, cp = _outproj_conv(yp, *mix_p, zero_hist, *cw, batch=batch, seq=seq, tm=tm)
        ys, cs = _outproj_conv(ys, *mix_s, caches[i + 1][0].astype(F32), *cw, batch=dec_batch, seq=dec_seq,
                               tm=dec_seq)
        outs += [cp, cs]
    return (yp.reshape(batch, seq, d), ys.reshape(dec_batch, dec_seq, d), *outs)
```

```python
import functools

import jax
import jax.numpy as jnp
from jax import lax
from jax.experimental import pallas as pl
from jax.experimental.pallas import tpu as pltpu

N_HEADS = 8
HEAD_DIM = 128
D_MODEL = 1024
CONV_K = 3
NORM_EPS = 1e-6
LOG2E = 1.4426950408889634
QK_SCALE_LOG2 = HEAD_DIM ** -0.5 * LOG2E
NEG_BIG = -1e30
FAST_LOGIT_BOUND = 60.0

LANES = 128
V7X_SCOPED_VMEM_BYTES = 60000 * 1024

F32 = jnp.float32
BF16 = jnp.bfloat16


def _params(semantics, vmem_bytes):
    return pltpu.CompilerParams(dimension_semantics=semantics,
                                vmem_limit_bytes=min(int(vmem_bytes), V7X_SCOPED_VMEM_BYTES))


def _rms_scale(x):
    return lax.rsqrt(jnp.mean(x * x, axis=-1, keepdims=True) + NORM_EPS)


def _dot(a, b):
    return jnp.dot(a, b, preferred_element_type=F32)


def _dot_nt(a, b):
    return lax.dot_general(a, b, (((1,), (1,)), ((), ())), preferred_element_type=F32)


def _silu(g):
    return g / (1.0 + jnp.exp(-g))


def _attn_inproj_kernel(x_ref, gn_ref, wq_ref, wk_ref, wv_ref, wg_ref, wf_ref, bf_ref, gq_ref, gk_ref,
                        q_ref, k_ref, kb_ref, v_ref, vb_ref, g_ref, lf_ref, lft_ref):
    x = x_ref[...]
    hb = (x * _rms_scale(x) * gn_ref[...]).astype(BF16)

    tm = x.shape[0]
    zq = _dot(hb, wq_ref[...])
    zk = _dot(hb, wk_ref[...])
    zv = _dot(hb, wv_ref[...])
    gq = gq_ref[...] * QK_SCALE_LOG2
    gk = gk_ref[...]
    for h in range(N_HEADS):
        sl = slice(h * HEAD_DIM, (h + 1) * HEAD_DIM)
        rows = pl.ds(h, tm, stride=N_HEADS)
        qh = zq[:, sl]
        q_ref[:, sl] = (qh * _rms_scale(qh) * gq).astype(BF16)
        kh = zk[:, sl]
        kn = kh * _rms_scale(kh) * gk
        k_ref[rows, :] = kn
        kb_ref[:, sl] = kn.astype(BF16)
        v_ref[rows, :] = zv[:, sl]

    vb_ref[...] = zv.astype(BF16)
    g_ref[...] = _dot(hb, wg_ref[...]).astype(BF16)

    zf = _dot(hb, wf_ref[...]) + bf_ref[...]
    lf = jnp.minimum(zf, 0.0) - jnp.log1p(jnp.exp(-jnp.abs(zf)))
    lf_ref[...] = lf[:, :N_HEADS]
    lft_ref[...] = jnp.transpose(lf)[:N_HEADS, :]


def _attn_inproj(x, gn, wq, wk, wv, wg, wf, bf, gq, gk, tm):
    n = x.shape[0]
    assert n % tm == 0
    row = lambda c: pl.BlockSpec((tm, c), lambda i: (i, 0))
    head_rows = pl.BlockSpec((tm * N_HEADS, HEAD_DIM), lambda i: (i, 0))
    full = lambda a: pl.BlockSpec(a.shape, lambda i: (0, 0))
    w_bytes = 2 * 2 * (4 * D_MODEL * D_MODEL + D_MODEL * LANES)
    io_bytes = 2 * tm * D_MODEL * (4 + 2 + 4 + 2 + 4 + 2 + 2)
    tmp_bytes = 6 * tm * D_MODEL * 4
    return pl.pallas_call(
        _attn_inproj_kernel,
        grid=(n // tm,),
        in_specs=[row(D_MODEL), full(gn), full(wq), full(wk), full(wv), full(wg), full(wf), full(bf),
                  full(gq), full(gk)],
        out_specs=[row(D_MODEL), head_rows, row(D_MODEL), head_rows, row(D_MODEL), row(D_MODEL),
                   row(N_HEADS), pl.BlockSpec((N_HEADS, tm), lambda i: (0, i))],
        out_shape=[jax.ShapeDtypeStruct((n, D_MODEL), BF16),
                   jax.ShapeDtypeStruct((n * N_HEADS, HEAD_DIM), F32),
                   jax.ShapeDtypeStruct((n, D_MODEL), BF16),
                   jax.ShapeDtypeStruct((n * N_HEADS, HEAD_DIM), F32),
                   jax.ShapeDtypeStruct((n, D_MODEL), BF16),
                   jax.ShapeDtypeStruct((n, D_MODEL), BF16),
                   jax.ShapeDtypeStruct((n, N_HEADS), F32),
                   jax.ShapeDtypeStruct((N_HEADS, n), F32)],
        compiler_params=_params(("parallel",), w_bytes + io_bytes + tmp_bytes),
        name="attn_inproj",
    )(x, gn, wq, wk, wv, wg, wf, bf, gq, gk)


def _cumsum_kernel(x_ref, u_ref, o_ref, *, segment):
    rows, length = x_ref.shape
    u = u_ref[...]
    starts = range(0, length, segment)
    carries = [jnp.zeros((rows, 1), F32) for _ in starts]
    for j in range(segment // LANES):
        for g, start in enumerate(starts):
            sl = slice(start + j * LANES, start + (j + 1) * LANES)
            x = x_ref[:, sl]
            hi = x.astype(BF16)
            r1 = x - hi.astype(F32)
            mid = r1.astype(BF16)
            lo = (r1 - mid.astype(F32)).astype(BF16)
            c = (_dot(hi, u) + _dot(mid, u)) + _dot(lo, u) + carries[g]
            o_ref[:, sl] = c * LOG2E
            carries[g] = c[:, LANES - 1:LANES]


def _cumsum_lanes(x, segment=None):
    rows, length = x.shape
    segment = length if segment is None else segment
    assert segment % LANES == 0 and length % segment == 0 and rows % 8 == 0
    idx = jnp.arange(LANES)
    u = (idx[:, None] <= idx[None, :]).astype(BF16)
    return pl.pallas_call(
        functools.partial(_cumsum_kernel, segment=segment),
        out_shape=jax.ShapeDtypeStruct((rows, length), F32),
        compiler_params=_params((), 8 * rows * length * 4 + (1 << 22)),
        name="cumsum_lanes",
    )(x, u)


def _causal_mask(blk):
    r = lax.broadcasted_iota(jnp.int32, (blk, blk), 0)
    c = lax.broadcasted_iota(jnp.int32, (blk, blk), 1)
    return c <= r


def _flash_online_kernel(q_ref, k_ref, v_ref, c_ref, o_ref, *, blk):
    i = pl.program_id(2)
    q = q_ref[...]

    def step(j, carry, masked):
        m, l, acc = carry
        off = pl.multiple_of(j * blk, blk)
        kb = k_ref[pl.ds(off, blk), :]
        vb = v_ref[pl.ds(off, blk), :]
        s = _dot_nt(q, kb) - c_ref[j]
        if masked:
            s = jnp.where(_causal_mask(blk), s, NEG_BIG)
        m_new = jnp.maximum(m, jnp.max(s, axis=-1, keepdims=True))
        p = jnp.exp2(s - m_new)
        alpha = jnp.exp2(m - m_new)
        l = alpha * l + jnp.sum(p, axis=-1, keepdims=True)
        acc = alpha * acc + _dot(p.astype(BF16), vb)
        return m_new, l, acc

    init = (jnp.full((blk, 1), NEG_BIG, F32), jnp.zeros((blk, 1), F32), jnp.zeros((blk, HEAD_DIM), F32))
    carry = lax.fori_loop(0, i, lambda j, c: step(j, c, False), init)
    m, l, acc = step(i, carry, True)
    o_ref[...] = (acc / l).astype(o_ref.dtype)


VT_ROWS = HEAD_DIM + 16
FLASH_KEY_BLOCKS_PER_QUERY_BLOCK = 8


def _flash_bounded_kernel(q_ref, k_ref, v_ref, c_ref, o_ref, vt_ref, cs_ref, mask_ref, p_ref, acc_ref, *, tk, r):
    i = pl.program_id(2)
    tq = r * tk
    lane_tiles = tk // LANES

    @pl.when(i == 0)
    def _():
        sub = lax.broadcasted_iota(jnp.int32, (VT_ROWS - HEAD_DIM, tk), 0)
        ones_rows = jnp.where(sub == 0, 1.0, 0.0).astype(BF16)
        eye = (lax.broadcasted_iota(jnp.int32, (HEAD_DIM, HEAD_DIM), 0)
               == lax.broadcasted_iota(jnp.int32, (HEAD_DIM, HEAD_DIM), 1)).astype(BF16)
        for jb in range(v_ref.shape[0] // tk):
            vt_ref[jb, :HEAD_DIM, :] = _dot_nt(eye, v_ref[jb * tk:(jb + 1) * tk, :]).astype(BF16)
            vt_ref[jb, HEAD_DIM:, :] = ones_rows
            c_row = c_ref[jb]
            for n in range(lane_tiles):
                lanes = slice(n * LANES, (n + 1) * LANES)
                cs_ref[jb, lanes, :] = jnp.transpose(jnp.broadcast_to(c_row[:, lanes], (LANES, LANES)))
        key = lax.broadcasted_iota(jnp.int32, (tk, tk), 0)
        query = lax.broadcasted_iota(jnp.int32, (tk, tk), 1)
        mask_ref[0] = jnp.zeros((tk, tk), F32)
        mask_ref[1] = jnp.where(key <= query, 0.0, NEG_BIG)

    first = r * i
    c_t = jnp.concatenate([c_ref[first + a] for a in range(r)], axis=1)

    def keys(j):
        return k_ref[pl.ds(pl.multiple_of(j * tk, tk), tk), :]

    def logits_t(j, cols):
        c_s = jnp.concatenate([cs_ref[j]] * ((cols.stop - cols.start) // LANES), axis=1)
        return (_dot_nt(keys(j), q_ref[cols, :]) + c_t[:, cols]) - c_s

    def probs_t(slot, j, mask_index, first_part=0):
        for part in range(first_part, r):
            cols = slice(part * tk, (part + 1) * tk)
            z = logits_t(j, cols)
            if mask_index is not None and part == first_part:
                z = z + mask_ref[mask_index]
            p_ref[slot, :, cols] = jnp.exp2(z).astype(BF16)

    def accumulate(slot, j):
        for half in range(2):
            cols = slice(half * tq // 2, (half + 1) * tq // 2)
            acc_ref[:, cols] += _dot(vt_ref[j], p_ref[slot, :, cols])

    probs_t(0, 0, (i == 0).astype(jnp.int32))
    acc_ref[...] = jnp.zeros_like(acc_ref)

    def two_blocks(j):
        accumulate(0, j)
        probs_t(1, j + 1, None)
        accumulate(1, j + 1)
        probs_t(0, j + 2, (j + 2 == first).astype(jnp.int32))

    def trip(t, _):
        two_blocks(4 * t)
        two_blocks(4 * t + 2)
        return 0

    lax.fori_loop(0, first // 4, trip, 0)
    accumulate(0, first)
    for d in range(1, r):
        slot = d % 2
        probs_t(slot, first + d, 1, first_part=d)
        acc_ref[:, d * tk:] += _dot(vt_ref[first + d], p_ref[slot, :, d * tk:])
    acc = acc_ref[...]
    o_ref[...] = jnp.transpose(acc[:HEAD_DIM, :] / acc[HEAD_DIM:HEAD_DIM + 1, :]).astype(o_ref.dtype)


def _flash_prompt(q, kb, vb, c, batch, seq, tk, bounded):
    r = FLASH_KEY_BLOCKS_PER_QUERY_BLOCK if bounded else 1
    tq = r * tk
    nq = seq // tq
    assert seq % tq == 0 and (r == 1 or r % 4 == 0)
    qspec = pl.BlockSpec((tq, HEAD_DIM), lambda b, h, i: (b * nq + i, h))
    kvspec = pl.BlockSpec((seq, HEAD_DIM), lambda b, h, i: (b, h))
    cspec = pl.BlockSpec((None, seq // tk, 1, tk), lambda b, h, i: (h, b, 0, 0))
    vmem = 2 * 2 * seq * HEAD_DIM * 2 + 4 * tq * HEAD_DIM * 2 + 2 * seq * 4 + 8 * tq * tk * 4
    if bounded:
        body = functools.partial(_flash_bounded_kernel, tk=tk, r=r)
        scratch = [pltpu.VMEM((seq // tk, VT_ROWS, tk), BF16),
                   pltpu.VMEM((seq // tk, tk, LANES), F32),
                   pltpu.VMEM((2, tk, tk), F32),
                   pltpu.VMEM((2, tk, tq), BF16),
                   pltpu.VMEM((VT_ROWS, tq), F32)]
        vmem += seq * VT_ROWS * 2 + seq * LANES * 4 + 2 * tk * tk * 4 + 2 * tq * tk * 2 + tq * VT_ROWS * 4
    else:
        body = functools.partial(_flash_online_kernel, blk=tk)
        scratch = []
    return pl.pallas_call(
        body,
        grid=(batch, N_HEADS, nq),
        in_specs=[qspec, kvspec, kvspec, cspec],
        out_specs=qspec,
        out_shape=jax.ShapeDtypeStruct(q.shape, BF16),
        scratch_shapes=scratch,
        compiler_params=_params(("parallel", "parallel", "arbitrary"), vmem),
        name="flash_bounded" if bounded else "flash_online",
    )(q, kb, vb, c)


def _sample_attn_kernel(q_ref, ck_ref, cv_ref, kn_ref, vn_ref, cc_ref, cn_ref, o_ref, m_ref, l_ref, acc_ref,
                        *, chunk, t_new):
    step = pl.program_id(1)

    @pl.when(step == 0)
    def _():
        m_ref[...] = jnp.full_like(m_ref, NEG_BIG)
        l_ref[...] = jnp.zeros_like(l_ref)
        acc_ref[...] = jnp.zeros_like(acc_ref)

    def rescale(h, s):
        m_old = m_ref[h]
        m_new = jnp.maximum(m_old, jnp.max(s, axis=-1, keepdims=True))
        p = jnp.exp2(s - m_new)
        alpha = jnp.exp2(m_old - m_new)
        l_ref[h] = alpha * l_ref[h] + jnp.sum(p, axis=-1, keepdims=True)
        m_ref[h] = m_new
        return alpha, p.astype(BF16)

    def update(h, s, values):
        alpha, p = rescale(h, s)
        acc_ref[h] = alpha * acc_ref[h] + _dot(p, values)

    heads = range(N_HEADS)
    head_rows = [pl.ds(h, chunk, stride=N_HEADS) for h in heads]
    scores = [_dot_nt(q_ref[:, h * HEAD_DIM:(h + 1) * HEAD_DIM], ck_ref[head_rows[h], :].astype(BF16))
              - cc_ref[h:h + 1, :] for h in heads]
    scaled = [rescale(h, scores[h]) for h in heads]
    for h in heads:
        alpha, p = scaled[h]
        acc_ref[h] = alpha * acc_ref[h] + _dot(p, cv_ref[head_rows[h], :].astype(BF16))

    @pl.when(step == pl.num_programs(1) - 1)
    def _():
        r = lax.broadcasted_iota(jnp.int32, (t_new, t_new), 0)
        col = lax.broadcasted_iota(jnp.int32, (t_new, t_new), 1)
        for h in range(N_HEADS):
            sl = slice(h * HEAD_DIM, (h + 1) * HEAD_DIM)
            s = _dot_nt(q_ref[:, sl], kn_ref[:, sl]) - cn_ref[h:h + 1, :t_new]
            update(h, jnp.where(col <= r, s, NEG_BIG), vn_ref[:, sl])
            o_ref[:, sl] = (acc_ref[h] / l_ref[h]).astype(o_ref.dtype)


def _sample_attn(q, cache_k, cache_v, kb, vb, c_cache, c_new, batch, t_new, past, chunk):
    assert past % chunk == 0
    nspec = pl.BlockSpec((t_new, D_MODEL), lambda b, s: (b, 0))
    cachespec = pl.BlockSpec((None, chunk * N_HEADS, HEAD_DIM), lambda b, s: (b, s, 0))
    ccspec = pl.BlockSpec((None, None, N_HEADS, chunk), lambda b, s: (b, s, 0, 0))
    cnspec = pl.BlockSpec((None, N_HEADS, LANES), lambda b, s: (b, 0, 0))
    head_rows = (batch, past * N_HEADS, HEAD_DIM)
    vmem = 2 * 2 * chunk * N_HEADS * HEAD_DIM * 4 + 8 * chunk * HEAD_DIM * 4 + (1 << 22)
    return pl.pallas_call(
        functools.partial(_sample_attn_kernel, chunk=chunk, t_new=t_new),
        grid=(batch, past // chunk),
        in_specs=[nspec, cachespec, cachespec, nspec, nspec, ccspec, cnspec],
        out_specs=nspec,
        out_shape=jax.ShapeDtypeStruct(q.shape, BF16),
        scratch_shapes=[pltpu.VMEM((N_HEADS, t_new, 1), F32), pltpu.VMEM((N_HEADS, t_new, 1), F32),
                        pltpu.VMEM((N_HEADS, t_new, HEAD_DIM), F32)],
        compiler_params=_params(("parallel", "arbitrary"), vmem),
        name="sample_attn",
    )(q, cache_k.reshape(head_rows), cache_v.reshape(head_rows), kb, vb, c_cache, c_new)


def _outproj_conv_kernel(x_ref, g_ref, o_ref, wa_ref, hist_ref, gn_ref, wb_ref, wc_ref, wu_ref, wg_ref, cw_ref,
                         wo_ref, y_ref, st_ref, tail_ref, *, tm):
    @pl.when(pl.program_id(1) == 0)
    def _():
        tail_ref[...] = jnp.zeros_like(tail_ref)
        tail_ref[8 - (CONV_K - 1):, :] = hist_ref[...]

    gated = _silu(g_ref[...].astype(F32)) * o_ref[...].astype(F32)
    x = x_ref[...] + _dot(gated.astype(BF16), wa_ref[...])
    hb = (x * _rms_scale(x) * gn_ref[...]).astype(BF16)
    cu = _dot(hb, wc_ref[...]) * _dot(hb, wu_ref[...])

    tail = tail_ref[...]
    h_m2 = tail[6:7, :]
    h_m1 = tail[7:8, :]
    r = lax.broadcasted_iota(jnp.int32, cu.shape, 0)
    s1 = jnp.where(r == 0, h_m1, pltpu.roll(cu, 1, 0))
    s2 = jnp.where(r == 0, h_m2, jnp.where(r == 1, h_m1, pltpu.roll(cu, 2, 0)))
    cw = cw_ref[...]
    conv = cw[0:1, :] * s2 + cw[1:2, :] * s1 + cw[2:3, :] * cu

    tail_ref[...] = cu[tm - 8:, :]
    st_ref[...] = cu[tm - (CONV_K - 1):, :]

    mixed = _silu(_dot(hb, wg_ref[...])) * _dot(hb, wb_ref[...]) * conv
    y_ref[...] = x + _dot(mixed.astype(BF16), wo_ref[...])


def _outproj_conv(x, gate, o, wa, hist, gn, wb, wc, wu, wg, cw, wo, batch, seq, tm):
    assert seq % tm == 0 and tm % 8 == 0 and tm >= 8
    nt = seq // tm
    row = pl.BlockSpec((tm, D_MODEL), lambda b, t: (b * nt + t, 0))
    st = pl.BlockSpec((None, CONV_K - 1, D_MODEL), lambda b, t: (b, 0, 0))
    full = lambda a: pl.BlockSpec(a.shape, lambda b, t: (0, 0))
    w_bytes = 2 * 2 * 6 * D_MODEL * D_MODEL
    io_bytes = 2 * tm * D_MODEL * (4 + 2 + 2 + 4)
    tmp_bytes = 10 * tm * D_MODEL * 4
    return pl.pallas_call(
        functools.partial(_outproj_conv_kernel, tm=tm),
        grid=(batch, nt),
        in_specs=[row, row, row, full(wa), st, full(gn), full(wb), full(wc), full(wu), full(wg), full(cw),
                  full(wo)],
        out_specs=[row, st],
        out_shape=[jax.ShapeDtypeStruct(x.shape, F32),
                   jax.ShapeDtypeStruct((batch, CONV_K - 1, D_MODEL), F32)],
        scratch_shapes=[pltpu.VMEM((8, D_MODEL), F32)],
        compiler_params=_params(("parallel", "arbitrary"), w_bytes + io_bytes + tmp_bytes),
        name="outproj_conv",
    )(x, gate, o, wa, hist, gn, wb, wc, wu, wg, cw, wo)


def _row(v, width=None):
    v = v.astype(F32).reshape(1, -1)
    if width is not None and v.shape[1] < width:
        v = jnp.pad(v, ((0, 0), (0, width - v.shape[1])))
    return v


def _attn_weights(g_norm, w_in, b_f, g_q, g_k, w_out):
    aw = N_HEADS * HEAD_DIM
    wq, wk, wv, wg = (w_in[:, i * aw:(i + 1) * aw].astype(BF16) for i in range(4))
    wf = jnp.pad(w_in[:, 4 * aw:], ((0, 0), (0, LANES - N_HEADS))).astype(BF16)
    return (_row(g_norm), wq, wk, wv, wg, wf, _row(b_f, LANES), _row(g_q), _row(g_k)), w_out.astype(BF16)


def _heads_to_rows(lf, batch, length):
    return jnp.transpose(lf.reshape(batch, length, N_HEADS), (0, 2, 1)).reshape(batch * N_HEADS, length)


def _attn_prompt_layer(x, params, batch, seq, tm, tk):
    inw, wo = _attn_weights(*params)
    q, k, kb, v, vb, gate, lf, lf_t = _attn_inproj(x, *inw, tm=tm)
    c = _cumsum_lanes(lf_t, segment=seq).reshape(N_HEADS, batch * (seq // tk), 1, tk)
    g_q, g_k = params[3], params[4]
    logit_bound = 1.01 * HEAD_DIM ** 0.5 * jnp.max(jnp.abs(g_q)) * jnp.max(jnp.abs(g_k))
    o = lax.cond(logit_bound < FAST_LOGIT_BOUND,
                 functools.partial(_flash_prompt, batch=batch, seq=seq, tk=tk, bounded=True),
                 functools.partial(_flash_prompt, batch=batch, seq=seq, tk=tk, bounded=False),
                 q, kb, vb, c)
    return (gate, o, wo), k, v, lf


def _attn_sample_layer(x, cache_k, cache_v, cache_logf, params, batch, t_new):
    past = cache_k.shape[1]
    inw, wo = _attn_weights(*params)
    q, k, kb, v, vb, gate, lf, _ = _attn_inproj(x, *inw, tm=x.shape[0])
    lf_all = jnp.concatenate([cache_logf.astype(F32), lf.reshape(batch, t_new, N_HEADS)], axis=1)
    total = past + t_new
    padded = -(-total // LANES) * LANES
    lf_rows = _heads_to_rows(lf_all.reshape(batch * total, N_HEADS), batch, total)
    lf_rows = jnp.pad(lf_rows, ((0, 0), (0, padded - total)))
    c = _cumsum_lanes(lf_rows).reshape(batch, N_HEADS, padded)
    chunk = 1024
    c_cache = jnp.transpose(c[:, :, :past].reshape(batch, N_HEADS, past // chunk, chunk), (0, 2, 1, 3))
    c_new = c[:, :, past:past + LANES]
    o = _sample_attn(q, cache_k, cache_v, kb, vb, c_cache, c_new, batch, t_new, past, chunk)
    return (gate, o, wo), k, v, lf


def _conv_weights(g_norm, w_in, conv_w, w_out):
    w = D_MODEL
    wb, wc, wu, wg = (w_in[:, i * w:(i + 1) * w].astype(BF16) for i in range(4))
    return _row(g_norm), wb, wc, wu, wg, conv_w.astype(F32), w_out.astype(BF16)


def kernel(x_prompt, x_sample, cache_k_l0, cache_v_l0, cache_logf_l0, state_conv_l1, cache_k_l2, cache_v_l2, cache_logf_l2, state_conv_l3, norm_l0, w_in_l0, b_f_l0, qnorm_l0, knorm_l0, w_out_l0, norm_l1, w_in_l1, conv_w_l1, w_out_l1, norm_l2, w_in_l2, b_f_l2, qnorm_l2, knorm_l2, w_out_l2, norm_l3, w_in_l3, conv_w_l3, w_out_l3):
    batch, seq, d = x_prompt.shape
    dec_batch, dec_seq, _ = x_sample.shape
    caches = [(cache_k_l0, cache_v_l0, cache_logf_l0), (state_conv_l1,),
              (cache_k_l2, cache_v_l2, cache_logf_l2), (state_conv_l3,)]
    params = [(norm_l0, w_in_l0, b_f_l0, qnorm_l0, knorm_l0, w_out_l0),
              (norm_l1, w_in_l1, conv_w_l1, w_out_l1),
              (norm_l2, w_in_l2, b_f_l2, qnorm_l2, knorm_l2, w_out_l2),
              (norm_l3, w_in_l3, conv_w_l3, w_out_l3)]
    tm, tk = 512, 512
    yp = x_prompt.reshape(batch * seq, d)
    ys = x_sample.reshape(dec_batch * dec_seq, d)
    outs = []
    for i in range(0, len(params), 2):
        mix_p, kp, vp, lfp = _attn_prompt_layer(yp, params[i], batch, seq, tm, tk)
        mix_s, ks, vs, lfs = _attn_sample_layer(ys, *caches[i], params[i], dec_batch, dec_seq)
        outs += [kp.reshape(batch, seq, N_HEADS, HEAD_DIM), vp.reshape(batch, seq, N_HEADS, HEAD_DIM),
                 lfp.reshape(batch, seq, N_HEADS),
                 ks.reshape(dec_batch, dec_seq, N_HEADS, HEAD_DIM),
                 vs.reshape(dec_batch, dec_seq, N_HEADS, HEAD_DIM),
                 lfs.reshape(dec_batch, dec_seq, N_HEADS)]
        cw = _conv_weights(*params[i + 1])
        zero_hist = jnp.zeros((batch, CONV_K - 1, d), F32)
        yp, cp = _outproj_conv(yp, *mix_p, zero_hist, *cw, batch=batch, seq=seq, tm=tm)
        ys, cs = _outproj_conv(ys, *mix_s, caches[i + 1][0].astype(F32), *cw, batch=dec_batch, seq=dec_seq,
                               tm=dec_seq)
        outs += [cp, cs]
    return (yp.reshape(batch, seq, d), ys.reshape(dec_batch, dec_seq, d), *outs)
```

```python
import functools

import jax
import jax.numpy as jnp
from jax import lax
from jax.experimental import pallas as pl
from jax.experimental.pallas import tpu as pltpu

N_HEADS = 8
HEAD_DIM = 128
D_MODEL = 1024
CONV_K = 3
NORM_EPS = 1e-6
LOG2E = 1.4426950408889634
QK_SCALE_LOG2 = HEAD_DIM ** -0.5 * LOG2E
NEG_BIG = -1e30
FAST_LOGIT_BOUND = 60.0

LANES = 128
V7X_SCOPED_VMEM_BYTES = 60000 * 1024

F32 = jnp.float32
BF16 = jnp.bfloat16


def _params(semantics, vmem_bytes):
    return pltpu.CompilerParams(dimension_semantics=semantics,
                                vmem_limit_bytes=min(int(vmem_bytes), V7X_SCOPED_VMEM_BYTES))


def _rms_scale(x):
    return lax.rsqrt(jnp.mean(x * x, axis=-1, keepdims=True) + NORM_EPS)


def _dot(a, b):
    return jnp.dot(a, b, preferred_element_type=F32)


def _dot_nt(a, b):
    return lax.dot_general(a, b, (((1,), (1,)), ((), ())), preferred_element_type=F32)


def _silu(g):
    return g / (1.0 + jnp.exp(-g))


def _attn_inproj_kernel(x_ref, gn_ref, wq_ref, wk_ref, wv_ref, wg_ref, wf_ref, bf_ref, gq_ref, gk_ref,
                        q_ref, k_ref, kb_ref, v_ref, vb_ref, g_ref, lf_ref):
    x = x_ref[...]
    hb = (x * _rms_scale(x) * gn_ref[...]).astype(BF16)

    tm = x.shape[0]
    zq = _dot(hb, wq_ref[...])
    zk = _dot(hb, wk_ref[...])
    zv = _dot(hb, wv_ref[...])
    gq = gq_ref[...] * QK_SCALE_LOG2
    gk = gk_ref[...]
    for h in range(N_HEADS):
        sl = slice(h * HEAD_DIM, (h + 1) * HEAD_DIM)
        rows = pl.ds(h, tm, stride=N_HEADS)
        qh = zq[:, sl]
        q_ref[:, sl] = (qh * _rms_scale(qh) * gq).astype(BF16)
        kh = zk[:, sl]
        kn = kh * _rms_scale(kh) * gk
        k_ref[rows, :] = kn
        kb_ref[:, sl] = kn.astype(BF16)
        v_ref[rows, :] = zv[:, sl]

    vb_ref[...] = zv.astype(BF16)
    g_ref[...] = _dot(hb, wg_ref[...]).astype(BF16)

    zf = _dot(hb, wf_ref[...]) + bf_ref[...]
    lf = jnp.minimum(zf, 0.0) - jnp.log1p(jnp.exp(-jnp.abs(zf)))
    lf_ref[...] = lf[:, :N_HEADS]


def _attn_inproj(x, gn, wq, wk, wv, wg, wf, bf, gq, gk, tm):
    n = x.shape[0]
    assert n % tm == 0
    row = lambda c: pl.BlockSpec((tm, c), lambda i: (i, 0))
    head_rows = pl.BlockSpec((tm * N_HEADS, HEAD_DIM), lambda i: (i, 0))
    full = lambda a: pl.BlockSpec(a.shape, lambda i: (0, 0))
    w_bytes = 2 * 2 * (4 * D_MODEL * D_MODEL + D_MODEL * LANES)
    io_bytes = 2 * tm * D_MODEL * (4 + 2 + 4 + 2 + 4 + 2 + 2)
    tmp_bytes = 6 * tm * D_MODEL * 4
    return pl.pallas_call(
        _attn_inproj_kernel,
        grid=(n // tm,),
        in_specs=[row(D_MODEL), full(gn), full(wq), full(wk), full(wv), full(wg), full(wf), full(bf),
                  full(gq), full(gk)],
        out_specs=[row(D_MODEL), head_rows, row(D_MODEL), head_rows, row(D_MODEL), row(D_MODEL),
                   row(N_HEADS)],
        out_shape=[jax.ShapeDtypeStruct((n, D_MODEL), BF16),
                   jax.ShapeDtypeStruct((n * N_HEADS, HEAD_DIM), F32),
                   jax.ShapeDtypeStruct((n, D_MODEL), BF16),
                   jax.ShapeDtypeStruct((n * N_HEADS, HEAD_DIM), F32),
                   jax.ShapeDtypeStruct((n, D_MODEL), BF16),
                   jax.ShapeDtypeStruct((n, D_MODEL), BF16),
                   jax.ShapeDtypeStruct((n, N_HEADS), F32)],
        compiler_params=_params(("parallel",), w_bytes + io_bytes + tmp_bytes),
        name="attn_inproj",
    )(x, gn, wq, wk, wv, wg, wf, bf, gq, gk)


def _cumsum_kernel(x_ref, u_ref, o_ref):
    rows, length = x_ref.shape
    u = u_ref[...]
    carry = jnp.zeros((rows, 1), F32)
    for j in range(length // LANES):
        sl = slice(j * LANES, (j + 1) * LANES)
        x = x_ref[:, sl]
        hi = x.astype(BF16)
        r1 = x - hi.astype(F32)
        mid = r1.astype(BF16)
        lo = (r1 - mid.astype(F32)).astype(BF16)
        c = (_dot(hi, u) + _dot(mid, u)) + _dot(lo, u) + carry
        o_ref[:, sl] = c * LOG2E
        carry = c[:, LANES - 1:LANES]


def _cumsum_lanes(x):
    rows, length = x.shape
    assert length % LANES == 0 and rows % 8 == 0
    idx = jnp.arange(LANES)
    u = (idx[:, None] <= idx[None, :]).astype(BF16)
    return pl.pallas_call(
        _cumsum_kernel,
        out_shape=jax.ShapeDtypeStruct((rows, length), F32),
        compiler_params=_params((), 8 * rows * length * 4 + (1 << 22)),
        name="cumsum_lanes",
    )(x, u)


def _causal_mask(blk):
    r = lax.broadcasted_iota(jnp.int32, (blk, blk), 0)
    c = lax.broadcasted_iota(jnp.int32, (blk, blk), 1)
    return c <= r


def _flash_online_kernel(q_ref, k_ref, v_ref, c_ref, o_ref, *, blk):
    i = pl.program_id(2)
    q = q_ref[...]

    def step(j, carry, masked):
        m, l, acc = carry
        off = pl.multiple_of(j * blk, blk)
        kb = k_ref[pl.ds(off, blk), :]
        vb = v_ref[pl.ds(off, blk), :]
        s = _dot_nt(q, kb) - c_ref[j]
        if masked:
            s = jnp.where(_causal_mask(blk), s, NEG_BIG)
        m_new = jnp.maximum(m, jnp.max(s, axis=-1, keepdims=True))
        p = jnp.exp2(s - m_new)
        alpha = jnp.exp2(m - m_new)
        l = alpha * l + jnp.sum(p, axis=-1, keepdims=True)
        acc = alpha * acc + _dot(p.astype(BF16), vb)
        return m_new, l, acc

    init = (jnp.full((blk, 1), NEG_BIG, F32), jnp.zeros((blk, 1), F32), jnp.zeros((blk, HEAD_DIM), F32))
    carry = lax.fori_loop(0, i, lambda j, c: step(j, c, False), init)
    m, l, acc = step(i, carry, True)
    o_ref[...] = (acc / l).astype(o_ref.dtype)


VT_ROWS = HEAD_DIM + 16
FLASH_KEY_BLOCKS_PER_QUERY_BLOCK = 8


def _flash_bounded_kernel(q_ref, k_ref, v_ref, c_ref, o_ref, vt_ref, cs_ref, mask_ref, p_ref, acc_ref, *, tk, r):
    i = pl.program_id(2)
    tq = r * tk
    lane_tiles = tk // LANES

    @pl.when(i == 0)
    def _():
        sub = lax.broadcasted_iota(jnp.int32, (VT_ROWS - HEAD_DIM, tk), 0)
        ones_rows = jnp.where(sub == 0, 1.0, 0.0).astype(BF16)
        eye = (lax.broadcasted_iota(jnp.int32, (HEAD_DIM, HEAD_DIM), 0)
               == lax.broadcasted_iota(jnp.int32, (HEAD_DIM, HEAD_DIM), 1)).astype(BF16)
        for jb in range(v_ref.shape[0] // tk):
            vt_ref[jb, :HEAD_DIM, :] = _dot_nt(eye, v_ref[jb * tk:(jb + 1) * tk, :]).astype(BF16)
            vt_ref[jb, HEAD_DIM:, :] = ones_rows
            c_row = c_ref[jb]
            for n in range(lane_tiles):
                lanes = slice(n * LANES, (n + 1) * LANES)
                cs_ref[jb, lanes, :] = jnp.transpose(jnp.broadcast_to(c_row[:, lanes], (LANES, LANES)))
        key = lax.broadcasted_iota(jnp.int32, (tk, tk), 0)
        query = lax.broadcasted_iota(jnp.int32, (tk, tk), 1)
        mask_ref[0] = jnp.zeros((tk, tk), F32)
        mask_ref[1] = jnp.where(key <= query, 0.0, NEG_BIG)

    first = r * i
    c_t = jnp.concatenate([c_ref[first + a] for a in range(r)], axis=1)

    def keys(j):
        return k_ref[pl.ds(pl.multiple_of(j * tk, tk), tk), :]

    def logits_t(j, cols):
        c_s = jnp.concatenate([cs_ref[j]] * ((cols.stop - cols.start) // LANES), axis=1)
        return (_dot_nt(keys(j), q_ref[cols, :]) + c_t[:, cols]) - c_s

    def probs_t(slot, j, mask_index, first_part=0):
        for part in range(first_part, r):
            cols = slice(part * tk, (part + 1) * tk)
            z = logits_t(j, cols)
            if mask_index is not None and part == first_part:
                z = z + mask_ref[mask_index]
            p_ref[slot, :, cols] = jnp.exp2(z).astype(BF16)

    def accumulate(slot, j):
        for half in range(2):
            cols = slice(half * tq // 2, (half + 1) * tq // 2)
            acc_ref[:, cols] += _dot(vt_ref[j], p_ref[slot, :, cols])

    probs_t(0, 0, (i == 0).astype(jnp.int32))
    acc_ref[...] = jnp.zeros_like(acc_ref)

    def two_blocks(j):
        accumulate(0, j)
        probs_t(1, j + 1, None)
        accumulate(1, j + 1)
        probs_t(0, j + 2, (j + 2 == first).astype(jnp.int32))

    def trip(t, _):
        two_blocks(4 * t)
        two_blocks(4 * t + 2)
        return 0

    lax.fori_loop(0, first // 4, trip, 0)
    accumulate(0, first)
    for d in range(1, r):
        slot = d % 2
        probs_t(slot, first + d, 1, first_part=d)
        acc_ref[:, d * tk:] += _dot(vt_ref[first + d], p_ref[slot, :, d * tk:])
    acc = acc_ref[...]
    o_ref[...] = jnp.transpose(acc[:HEAD_DIM, :] / acc[HEAD_DIM:HEAD_DIM + 1, :]).astype(o_ref.dtype)


def _flash_prompt(q, kb, vb, c, batch, seq, tk, bounded):
    r = FLASH_KEY_BLOCKS_PER_QUERY_BLOCK if bounded else 1
    tq = r * tk
    nq = seq // tq
    assert seq % tq == 0 and (r == 1 or r % 4 == 0)
    qspec = pl.BlockSpec((tq, HEAD_DIM), lambda b, h, i: (b * nq + i, h))
    kvspec = pl.BlockSpec((seq, HEAD_DIM), lambda b, h, i: (b, h))
    cspec = pl.BlockSpec((None, seq // tk, 1, tk), lambda b, h, i: (b * N_HEADS + h, 0, 0, 0))
    vmem = 2 * 2 * seq * HEAD_DIM * 2 + 4 * tq * HEAD_DIM * 2 + 2 * seq * 4 + 8 * tq * tk * 4
    if bounded:
        body = functools.partial(_flash_bounded_kernel, tk=tk, r=r)
        scratch = [pltpu.VMEM((seq // tk, VT_ROWS, tk), BF16),
                   pltpu.VMEM((seq // tk, tk, LANES), F32),
                   pltpu.VMEM((2, tk, tk), F32),
                   pltpu.VMEM((2, tk, tq), BF16),
                   pltpu.VMEM((VT_ROWS, tq), F32)]
        vmem += seq * VT_ROWS * 2 + seq * LANES * 4 + 2 * tk * tk * 4 + 2 * tq * tk * 2 + tq * VT_ROWS * 4
    else:
        body = functools.partial(_flash_online_kernel, blk=tk)
        scratch = []
    return pl.pallas_call(
        body,
        grid=(batch, N_HEADS, nq),
        in_specs=[qspec, kvspec, kvspec, cspec],
        out_specs=qspec,
        out_shape=jax.ShapeDtypeStruct(q.shape, BF16),
        scratch_shapes=scratch,
        compiler_params=_params(("parallel", "parallel", "arbitrary"), vmem),
        name="flash_bounded" if bounded else "flash_online",
    )(q, kb, vb, c)


def _sample_attn_kernel(q_ref, ck_ref, cv_ref, kn_ref, vn_ref, cc_ref, cn_ref, o_ref, m_ref, l_ref, acc_ref,
                        *, chunk, t_new):
    step = pl.program_id(1)

    @pl.when(step == 0)
    def _():
        m_ref[...] = jnp.full_like(m_ref, NEG_BIG)
        l_ref[...] = jnp.zeros_like(l_ref)
        acc_ref[...] = jnp.zeros_like(acc_ref)

    def rescale(h, s):
        m_old = m_ref[h]
        m_new = jnp.maximum(m_old, jnp.max(s, axis=-1, keepdims=True))
        p = jnp.exp2(s - m_new)
        alpha = jnp.exp2(m_old - m_new)
        l_ref[h] = alpha * l_ref[h] + jnp.sum(p, axis=-1, keepdims=True)
        m_ref[h] = m_new
        return alpha, p.astype(BF16)

    def update(h, s, values):
        alpha, p = rescale(h, s)
        acc_ref[h] = alpha * acc_ref[h] + _dot(p, values)

    heads = range(N_HEADS)
    head_rows = [pl.ds(h, chunk, stride=N_HEADS) for h in heads]
    scores = [_dot_nt(q_ref[:, h * HEAD_DIM:(h + 1) * HEAD_DIM], ck_ref[head_rows[h], :].astype(BF16))
              - cc_ref[h:h + 1, :] for h in heads]
    scaled = [rescale(h, scores[h]) for h in heads]
    for h in heads:
        alpha, p = scaled[h]
        acc_ref[h] = alpha * acc_ref[h] + _dot(p, cv_ref[head_rows[h], :].astype(BF16))

    @pl.when(step == pl.num_programs(1) - 1)
    def _():
        r = lax.broadcasted_iota(jnp.int32, (t_new, t_new), 0)
        col = lax.broadcasted_iota(jnp.int32, (t_new, t_new), 1)
        for h in range(N_HEADS):
            sl = slice(h * HEAD_DIM, (h + 1) * HEAD_DIM)
            s = _dot_nt(q_ref[:, sl], kn_ref[:, sl]) - cn_ref[h:h + 1, :t_new]
            update(h, jnp.where(col <= r, s, NEG_BIG), vn_ref[:, sl])
            o_ref[:, sl] = (acc_ref[h] / l_ref[h]).astype(o_ref.dtype)


def _sample_attn(q, cache_k, cache_v, kb, vb, c_cache, c_new, batch, t_new, past, chunk):
    assert past % chunk == 0
    nspec = pl.BlockSpec((t_new, D_MODEL), lambda b, s: (b, 0))
    cachespec = pl.BlockSpec((None, chunk * N_HEADS, HEAD_DIM), lambda b, s: (b, s, 0))
    ccspec = pl.BlockSpec((None, None, N_HEADS, chunk), lambda b, s: (b, s, 0, 0))
    cnspec = pl.BlockSpec((None, N_HEADS, LANES), lambda b, s: (b, 0, 0))
    head_rows = (batch, past * N_HEADS, HEAD_DIM)
    vmem = 2 * 2 * chunk * N_HEADS * HEAD_DIM * 4 + 8 * chunk * HEAD_DIM * 4 + (1 << 22)
    return pl.pallas_call(
        functools.partial(_sample_attn_kernel, chunk=chunk, t_new=t_new),
        grid=(batch, past // chunk),
        in_specs=[nspec, cachespec, cachespec, nspec, nspec, ccspec, cnspec],
        out_specs=nspec,
        out_shape=jax.ShapeDtypeStruct(q.shape, BF16),
        scratch_shapes=[pltpu.VMEM((N_HEADS, t_new, 1), F32), pltpu.VMEM((N_HEADS, t_new, 1), F32),
                        pltpu.VMEM((N_HEADS, t_new, HEAD_DIM), F32)],
        compiler_params=_params(("parallel", "arbitrary"), vmem),
        name="sample_attn",
    )(q, cache_k.reshape(head_rows), cache_v.reshape(head_rows), kb, vb, c_cache, c_new)


def _outproj_conv_kernel(x_ref, g_ref, o_ref, wa_ref, hist_ref, gn_ref, wb_ref, wc_ref, wu_ref, wg_ref, cw_ref,
                         wo_ref, y_ref, st_ref, tail_ref, *, tm):
    @pl.when(pl.program_id(1) == 0)
    def _():
        tail_ref[...] = jnp.zeros_like(tail_ref)
        tail_ref[8 - (CONV_K - 1):, :] = hist_ref[...]

    gated = _silu(g_ref[...].astype(F32)) * o_ref[...].astype(F32)
    x = x_ref[...] + _dot(gated.astype(BF16), wa_ref[...])
    hb = (x * _rms_scale(x) * gn_ref[...]).astype(BF16)
    cu = _dot(hb, wc_ref[...]) * _dot(hb, wu_ref[...])

    tail = tail_ref[...]
    h_m2 = tail[6:7, :]
    h_m1 = tail[7:8, :]
    r = lax.broadcasted_iota(jnp.int32, cu.shape, 0)
    s1 = jnp.where(r == 0, h_m1, pltpu.roll(cu, 1, 0))
    s2 = jnp.where(r == 0, h_m2, jnp.where(r == 1, h_m1, pltpu.roll(cu, 2, 0)))
    cw = cw_ref[...]
    conv = cw[0:1, :] * s2 + cw[1:2, :] * s1 + cw[2:3, :] * cu

    tail_ref[...] = cu[tm - 8:, :]
    st_ref[...] = cu[tm - (CONV_K - 1):, :]

    mixed = _silu(_dot(hb, wg_ref[...])) * _dot(hb, wb_ref[...]) * conv
    y_ref[...] = x + _dot(mixed.astype(BF16), wo_ref[...])


def _outproj_conv(x, gate, o, wa, hist, gn, wb, wc, wu, wg, cw, wo, batch, seq, tm):
    assert seq % tm == 0 and tm % 8 == 0 and tm >= 8
    nt = seq // tm
    row = pl.BlockSpec((tm, D_MODEL), lambda b, t: (b * nt + t, 0))
    st = pl.BlockSpec((None, CONV_K - 1, D_MODEL), lambda b, t: (b, 0, 0))
    full = lambda a: pl.BlockSpec(a.shape, lambda b, t: (0, 0))
    w_bytes = 2 * 2 * 6 * D_MODEL * D_MODEL
    io_bytes = 2 * tm * D_MODEL * (4 + 2 + 2 + 4)
    tmp_bytes = 10 * tm * D_MODEL * 4
    return pl.pallas_call(
        functools.partial(_outproj_conv_kernel, tm=tm),
        grid=(batch, nt),
        in_specs=[row, row, row, full(wa), st, full(gn), full(wb), full(wc), full(wu), full(wg), full(cw),
                  full(wo)],
        out_specs=[row, st],
        out_shape=[jax.ShapeDtypeStruct(x.shape, F32),
                   jax.ShapeDtypeStruct((batch, CONV_K - 1, D_MODEL), F32)],
        scratch_shapes=[pltpu.VMEM((8, D_MODEL), F32)],
        compiler_params=_params(("parallel", "arbitrary"), w_bytes + io_bytes + tmp_bytes),
        name="outproj_conv",
    )(x, gate, o, wa, hist, gn, wb, wc, wu, wg, cw, wo)


def _row(v, width=None):
    v = v.astype(F32).reshape(1, -1)
    if width is not None and v.shape[1] < width:
        v = jnp.pad(v, ((0, 0), (0, width - v.shape[1])))
    return v


def _attn_weights(g_norm, w_in, b_f, g_q, g_k, w_out):
    aw = N_HEADS * HEAD_DIM
    wq, wk, wv, wg = (w_in[:, i * aw:(i + 1) * aw].astype(BF16) for i in range(4))
    wf = jnp.pad(w_in[:, 4 * aw:], ((0, 0), (0, LANES - N_HEADS))).astype(BF16)
    return (_row(g_norm), wq, wk, wv, wg, wf, _row(b_f, LANES), _row(g_q), _row(g_k)), w_out.astype(BF16)


def _heads_to_rows(lf, batch, length):
    return jnp.transpose(lf.reshape(batch, length, N_HEADS), (0, 2, 1)).reshape(batch * N_HEADS, length)


def _attn_prompt_layer(x, params, batch, seq, tm, tk):
    inw, wo = _attn_weights(*params)
    q, k, kb, v, vb, gate, lf = _attn_inproj(x, *inw, tm=tm)
    c = _cumsum_lanes(_heads_to_rows(lf, batch, seq))
    c = c.reshape(batch * N_HEADS, seq // tk, 1, tk)
    g_q, g_k = params[3], params[4]
    logit_bound = 1.01 * HEAD_DIM ** 0.5 * jnp.max(jnp.abs(g_q)) * jnp.max(jnp.abs(g_k))
    o = lax.cond(logit_bound < FAST_LOGIT_BOUND,
                 functools.partial(_flash_prompt, batch=batch, seq=seq, tk=tk, bounded=True),
                 functools.partial(_flash_prompt, batch=batch, seq=seq, tk=tk, bounded=False),
                 q, kb, vb, c)
    return (gate, o, wo), k, v, lf


def _attn_sample_layer(x, cache_k, cache_v, cache_logf, params, batch, t_new):
    past = cache_k.shape[1]
    inw, wo = _attn_weights(*params)
    q, k, kb, v, vb, gate, lf = _attn_inproj(x, *inw, tm=x.shape[0])
    lf_all = jnp.concatenate([cache_logf.astype(F32), lf.reshape(batch, t_new, N_HEADS)], axis=1)
    total = past + t_new
    padded = -(-total // LANES) * LANES
    lf_rows = _heads_to_rows(lf_all.reshape(batch * total, N_HEADS), batch, total)
    lf_rows = jnp.pad(lf_rows, ((0, 0), (0, padded - total)))
    c = _cumsum_lanes(lf_rows).reshape(batch, N_HEADS, padded)
    chunk = 2048
    c_cache = jnp.transpose(c[:, :, :past].reshape(batch, N_HEADS, past // chunk, chunk), (0, 2, 1, 3))
    c_new = c[:, :, past:past + LANES]
    o = _sample_attn(q, cache_k, cache_v, kb, vb, c_cache, c_new, batch, t_new, past, chunk)
    return (gate, o, wo), k, v, lf


def _conv_weights(g_norm, w_in, conv_w, w_out):
    w = D_MODEL
    wb, wc, wu, wg = (w_in[:, i * w:(i + 1) * w].astype(BF16) for i in range(4))
    return _row(g_norm), wb, wc, wu, wg, conv_w.astype(F32), w_out.astype(BF16)


def kernel(x_prompt, x_sample, cache_k_l0, cache_v_l0, cache_logf_l0, state_conv_l1, cache_k_l2, cache_v_l2, cache_logf_l2, state_conv_l3, norm_l0, w_in_l0, b_f_l0, qnorm_l0, knorm_l0, w_out_l0, norm_l1, w_in_l1, conv_w_l1, w_out_l1, norm_l2, w_in_l2, b_f_l2, qnorm_l2, knorm_l2, w_out_l2, norm_l3, w_in_l3, conv_w_l3, w_out_l3):
    batch, seq, d = x_prompt.shape
    dec_batch, dec_seq, _ = x_sample.shape
    caches = [(cache_k_l0, cache_v_l0, cache_logf_l0), (state_conv_l1,),
              (cache_k_l2, cache_v_l2, cache_logf_l2), (state_conv_l3,)]
    params = [(norm_l0, w_in_l0, b_f_l0, qnorm_l0, knorm_l0, w_out_l0),
              (norm_l1, w_in_l1, conv_w_l1, w_out_l1),
              (norm_l2, w_in_l2, b_f_l2, qnorm_l2, knorm_l2, w_out_l2),
              (norm_l3, w_in_l3, conv_w_l3, w_out_l3)]
    tm, tk = 512, 512
    yp = x_prompt.reshape(batch * seq, d)
    ys = x_sample.reshape(dec_batch * dec_seq, d)
    outs = []
    for i in range(0, len(params), 2):
        mix_p, kp, vp, lfp = _attn_prompt_layer(yp, params[i], batch, seq, tm, tk)
        mix_s, ks, vs, lfs = _attn_sample_layer(ys, *caches[i], params[i], dec_batch, dec_seq)
        outs += [kp.reshape(batch, seq, N_HEADS, HEAD_DIM), vp.reshape(batch, seq, N_HEADS, HEAD_DIM),
                 lfp.reshape(batch, seq, N_HEADS),
                 ks.reshape(dec_batch, dec_seq, N_HEADS, HEAD_DIM),
                 vs.reshape(dec_batch, dec_seq, N_HEADS, HEAD_DIM),
                 lfs.reshape(dec_batch, dec_seq, N_HEADS)]
        cw = _conv_weights(*params[i + 1])
        zero_hist = jnp.zeros((batch, CONV_K - 1, d), F32)
        yp, cp = _outproj_conv(yp, *mix_p, zero_hist, *cw, batch=batch, seq=seq, tm=tm)
        ys, cs = _outproj_conv(ys, *mix_s, caches[i + 1][0].astype(F32), *cw, batch=dec_batch, seq=dec_seq,
                               tm=dec_seq)
        outs += [cp, cs]
    return (yp.reshape(batch, seq, d), ys.reshape(dec_batch, dec_seq, d), *outs)
```

```python
import functools

import jax
import jax.numpy as jnp
from jax import lax
from jax.experimental import pallas as pl
from jax.experimental.pallas import tpu as pltpu

N_HEADS = 8
HEAD_DIM = 128
D_MODEL = 1024
CONV_K = 3
NORM_EPS = 1e-6
LOG2E = 1.4426950408889634
QK_SCALE_LOG2 = HEAD_DIM ** -0.5 * LOG2E
NEG_BIG = -1e30
FAST_LOGIT_BOUND = 60.0

LANES = 128
V7X_SCOPED_VMEM_BYTES = 60000 * 1024

F32 = jnp.float32
BF16 = jnp.bfloat16


def _params(semantics, vmem_bytes):
    return pltpu.CompilerParams(dimension_semantics=semantics,
                                vmem_limit_bytes=min(int(vmem_bytes), V7X_SCOPED_VMEM_BYTES))


def _rms_scale(x):
    return lax.rsqrt(jnp.mean(x * x, axis=-1, keepdims=True) + NORM_EPS)


def _dot(a, b):
    return jnp.dot(a, b, preferred_element_type=F32)


def _dot_nt(a, b):
    return lax.dot_general(a, b, (((1,), (1,)), ((), ())), preferred_element_type=F32)


def _silu(g):
    return g / (1.0 + jnp.exp(-g))


def _attn_inproj_kernel(x_ref, gn_ref, wq_ref, wk_ref, wv_ref, wg_ref, wf_ref, bf_ref, gq_ref, gk_ref,
                        q_ref, k_ref, kb_ref, v_ref, vb_ref, g_ref, lf_ref):
    x = x_ref[...]
    hb = (x * _rms_scale(x) * gn_ref[...]).astype(BF16)

    tm = x.shape[0]
    zq = _dot(hb, wq_ref[...])
    zk = _dot(hb, wk_ref[...])
    zv = _dot(hb, wv_ref[...])
    gq = gq_ref[...] * QK_SCALE_LOG2
    gk = gk_ref[...]
    for h in range(N_HEADS):
        sl = slice(h * HEAD_DIM, (h + 1) * HEAD_DIM)
        rows = pl.ds(h, tm, stride=N_HEADS)
        qh = zq[:, sl]
        q_ref[:, sl] = (qh * _rms_scale(qh) * gq).astype(BF16)
        kh = zk[:, sl]
        kn = kh * _rms_scale(kh) * gk
        k_ref[rows, :] = kn
        kb_ref[:, sl] = kn.astype(BF16)
        v_ref[rows, :] = zv[:, sl]

    vb_ref[...] = zv.astype(BF16)
    g_ref[...] = _dot(hb, wg_ref[...]).astype(BF16)

    zf = _dot(hb, wf_ref[...]) + bf_ref[...]
    lf = jnp.minimum(zf, 0.0) - jnp.log1p(jnp.exp(-jnp.abs(zf)))
    lf_ref[...] = lf[:, :N_HEADS]


def _attn_inproj(x, gn, wq, wk, wv, wg, wf, bf, gq, gk, tm):
    n = x.shape[0]
    assert n % tm == 0
    row = lambda c: pl.BlockSpec((tm, c), lambda i: (i, 0))
    head_rows = pl.BlockSpec((tm * N_HEADS, HEAD_DIM), lambda i: (i, 0))
    full = lambda a: pl.BlockSpec(a.shape, lambda i: (0, 0))
    w_bytes = 2 * 2 * (4 * D_MODEL * D_MODEL + D_MODEL * LANES)
    io_bytes = 2 * tm * D_MODEL * (4 + 2 + 4 + 2 + 4 + 2 + 2)
    tmp_bytes = 6 * tm * D_MODEL * 4
    return pl.pallas_call(
        _attn_inproj_kernel,
        grid=(n // tm,),
        in_specs=[row(D_MODEL), full(gn), full(wq), full(wk), full(wv), full(wg), full(wf), full(bf),
                  full(gq), full(gk)],
        out_specs=[row(D_MODEL), head_rows, row(D_MODEL), head_rows, row(D_MODEL), row(D_MODEL),
                   row(N_HEADS)],
        out_shape=[jax.ShapeDtypeStruct((n, D_MODEL), BF16),
                   jax.ShapeDtypeStruct((n * N_HEADS, HEAD_DIM), F32),
                   jax.ShapeDtypeStruct((n, D_MODEL), BF16),
                   jax.ShapeDtypeStruct((n * N_HEADS, HEAD_DIM), F32),
                   jax.ShapeDtypeStruct((n, D_MODEL), BF16),
                   jax.ShapeDtypeStruct((n, D_MODEL), BF16),
                   jax.ShapeDtypeStruct((n, N_HEADS), F32)],
        compiler_params=_params(("parallel",), w_bytes + io_bytes + tmp_bytes),
        name="attn_inproj",
    )(x, gn, wq, wk, wv, wg, wf, bf, gq, gk)


def _cumsum_kernel(x_ref, u_ref, o_ref):
    rows, length = x_ref.shape
    u = u_ref[...]
    carry = jnp.zeros((rows, 1), F32)
    for j in range(length // LANES):
        sl = slice(j * LANES, (j + 1) * LANES)
        x = x_ref[:, sl]
        hi = x.astype(BF16)
        r1 = x - hi.astype(F32)
        mid = r1.astype(BF16)
        lo = (r1 - mid.astype(F32)).astype(BF16)
        c = (_dot(hi, u) + _dot(mid, u)) + _dot(lo, u) + carry
        o_ref[:, sl] = c * LOG2E
        carry = c[:, LANES - 1:LANES]


def _cumsum_lanes(x):
    rows, length = x.shape
    assert length % LANES == 0 and rows % 8 == 0
    idx = jnp.arange(LANES)
    u = (idx[:, None] <= idx[None, :]).astype(BF16)
    return pl.pallas_call(
        _cumsum_kernel,
        out_shape=jax.ShapeDtypeStruct((rows, length), F32),
        compiler_params=_params((), 8 * rows * length * 4 + (1 << 22)),
        name="cumsum_lanes",
    )(x, u)


def _causal_mask(blk):
    r = lax.broadcasted_iota(jnp.int32, (blk, blk), 0)
    c = lax.broadcasted_iota(jnp.int32, (blk, blk), 1)
    return c <= r


def _flash_online_kernel(q_ref, k_ref, v_ref, c_ref, o_ref, *, blk):
    i = pl.program_id(2)
    q = q_ref[...]

    def step(j, carry, masked):
        m, l, acc = carry
        off = pl.multiple_of(j * blk, blk)
        kb = k_ref[pl.ds(off, blk), :]
        vb = v_ref[pl.ds(off, blk), :]
        s = _dot_nt(q, kb) - c_ref[j]
        if masked:
            s = jnp.where(_causal_mask(blk), s, NEG_BIG)
        m_new = jnp.maximum(m, jnp.max(s, axis=-1, keepdims=True))
        p = jnp.exp2(s - m_new)
        alpha = jnp.exp2(m - m_new)
        l = alpha * l + jnp.sum(p, axis=-1, keepdims=True)
        acc = alpha * acc + _dot(p.astype(BF16), vb)
        return m_new, l, acc

    init = (jnp.full((blk, 1), NEG_BIG, F32), jnp.zeros((blk, 1), F32), jnp.zeros((blk, HEAD_DIM), F32))
    carry = lax.fori_loop(0, i, lambda j, c: step(j, c, False), init)
    m, l, acc = step(i, carry, True)
    o_ref[...] = (acc / l).astype(o_ref.dtype)


VT_ROWS = HEAD_DIM + 16
FLASH_KEY_BLOCKS_PER_QUERY_BLOCK = 8


def _flash_bounded_kernel(q_ref, k_ref, v_ref, c_ref, o_ref, vt_ref, cs_ref, mask_ref, p_ref, acc_ref, *, tk, r):
    i = pl.program_id(2)
    tq = r * tk
    lane_tiles = tk // LANES

    @pl.when(i == 0)
    def _():
        sub = lax.broadcasted_iota(jnp.int32, (VT_ROWS - HEAD_DIM, tk), 0)
        ones_rows = jnp.where(sub == 0, 1.0, 0.0).astype(BF16)
        eye = (lax.broadcasted_iota(jnp.int32, (HEAD_DIM, HEAD_DIM), 0)
               == lax.broadcasted_iota(jnp.int32, (HEAD_DIM, HEAD_DIM), 1)).astype(BF16)
        for jb in range(v_ref.shape[0] // tk):
            vt_ref[jb, :HEAD_DIM, :] = _dot_nt(eye, v_ref[jb * tk:(jb + 1) * tk, :]).astype(BF16)
            vt_ref[jb, HEAD_DIM:, :] = ones_rows
            c_row = c_ref[jb]
            for n in range(lane_tiles):
                lanes = slice(n * LANES, (n + 1) * LANES)
                cs_ref[jb, lanes, :] = jnp.transpose(jnp.broadcast_to(c_row[:, lanes], (LANES, LANES)))
        key = lax.broadcasted_iota(jnp.int32, (tk, tk), 0)
        query = lax.broadcasted_iota(jnp.int32, (tk, tk), 1)
        mask_ref[0] = jnp.zeros((tk, tk), F32)
        mask_ref[1] = jnp.where(key <= query, 0.0, NEG_BIG)

    first = r * i
    c_t = jnp.concatenate([c_ref[first + a] for a in range(r)], axis=1)

    def keys(j):
        return k_ref[pl.ds(pl.multiple_of(j * tk, tk), tk), :]

    def logits_t(j, cols):
        c_s = jnp.concatenate([cs_ref[j]] * ((cols.stop - cols.start) // LANES), axis=1)
        return (_dot_nt(keys(j), q_ref[cols, :]) + c_t[:, cols]) - c_s

    def probs_t(slot, j, mask_index, first_part=0):
        for part in range(first_part, r):
            cols = slice(part * tk, (part + 1) * tk)
            z = logits_t(j, cols)
            if mask_index is not None and part == first_part:
                z = z + mask_ref[mask_index]
            p_ref[slot, :, cols] = jnp.exp2(z).astype(BF16)

    def accumulate(slot, j):
        for half in range(2):
            cols = slice(half * tq // 2, (half + 1) * tq // 2)
            acc_ref[:, cols] += _dot(vt_ref[j], p_ref[slot, :, cols])

    probs_t(0, 0, (i == 0).astype(jnp.int32))
    acc_ref[...] = jnp.zeros_like(acc_ref)

    def two_blocks(j):
        accumulate(0, j)
        probs_t(1, j + 1, None)
        accumulate(1, j + 1)
        probs_t(0, j + 2, (j + 2 == first).astype(jnp.int32))

    def trip(t, _):
        two_blocks(4 * t)
        two_blocks(4 * t + 2)
        return 0

    lax.fori_loop(0, first // 4, trip, 0)
    accumulate(0, first)
    for d in range(1, r):
        slot = d % 2
        probs_t(slot, first + d, 1, first_part=d)
        acc_ref[:, d * tk:] += _dot(vt_ref[first + d], p_ref[slot, :, d * tk:])
    acc = acc_ref[...]
    o_ref[...] = jnp.transpose(acc[:HEAD_DIM, :] / acc[HEAD_DIM:HEAD_DIM + 1, :]).astype(o_ref.dtype)


def _flash_prompt(q, kb, vb, c, batch, seq, tk, bounded):
    r = FLASH_KEY_BLOCKS_PER_QUERY_BLOCK if bounded else 1
    tq = r * tk
    nq = seq // tq
    assert seq % tq == 0 and (r == 1 or r % 4 == 0)
    qspec = pl.BlockSpec((tq, HEAD_DIM), lambda b, h, i: (b * nq + i, h))
    kvspec = pl.BlockSpec((seq, HEAD_DIM), lambda b, h, i: (b, h))
    cspec = pl.BlockSpec((None, seq // tk, 1, tk), lambda b, h, i: (b * N_HEADS + h, 0, 0, 0))
    vmem = 2 * 2 * seq * HEAD_DIM * 2 + 4 * tq * HEAD_DIM * 2 + 2 * seq * 4 + 8 * tq * tk * 4
    if bounded:
        body = functools.partial(_flash_bounded_kernel, tk=tk, r=r)
        scratch = [pltpu.VMEM((seq // tk, VT_ROWS, tk), BF16),
                   pltpu.VMEM((seq // tk, tk, LANES), F32),
                   pltpu.VMEM((2, tk, tk), F32),
                   pltpu.VMEM((2, tk, tq), BF16),
                   pltpu.VMEM((VT_ROWS, tq), F32)]
        vmem += seq * VT_ROWS * 2 + seq * LANES * 4 + 2 * tk * tk * 4 + 2 * tq * tk * 2 + tq * VT_ROWS * 4
    else:
        body = functools.partial(_flash_online_kernel, blk=tk)
        scratch = []
    return pl.pallas_call(
        body,
        grid=(batch, N_HEADS, nq),
        in_specs=[qspec, kvspec, kvspec, cspec],
        out_specs=qspec,
        out_shape=jax.ShapeDtypeStruct(q.shape, BF16),
        scratch_shapes=scratch,
        compiler_params=_params(("parallel", "parallel", "arbitrary"), vmem),
        name="flash_bounded" if bounded else "flash_online",
    )(q, kb, vb, c)


def _sample_attn_kernel(q_ref, ck_ref, cv_ref, kn_ref, vn_ref, cc_ref, cn_ref, o_ref, m_ref, l_ref, acc_ref,
                        *, chunk, t_new):
    step = pl.program_id(1)

    @pl.when(step == 0)
    def _():
        m_ref[...] = jnp.full_like(m_ref, NEG_BIG)
        l_ref[...] = jnp.zeros_like(l_ref)
        acc_ref[...] = jnp.zeros_like(acc_ref)

    def rescale(h, s):
        m_old = m_ref[h]
        m_new = jnp.maximum(m_old, jnp.max(s, axis=-1, keepdims=True))
        p = jnp.exp2(s - m_new)
        alpha = jnp.exp2(m_old - m_new)
        l_ref[h] = alpha * l_ref[h] + jnp.sum(p, axis=-1, keepdims=True)
        m_ref[h] = m_new
        return alpha, p.astype(BF16)

    def update(h, s, values):
        alpha, p = rescale(h, s)
        acc_ref[h] = alpha * acc_ref[h] + _dot(p, values)

    heads = range(N_HEADS)
    head_rows = [pl.ds(h, chunk, stride=N_HEADS) for h in heads]
    scores = [_dot_nt(q_ref[:, h * HEAD_DIM:(h + 1) * HEAD_DIM], ck_ref[head_rows[h], :].astype(BF16))
              - cc_ref[h:h + 1, :] for h in heads]
    scaled = [rescale(h, scores[h]) for h in heads]
    for h in heads:
        alpha, p = scaled[h]
        acc_ref[h] = alpha * acc_ref[h] + _dot(p, cv_ref[head_rows[h], :].astype(BF16))

    @pl.when(step == pl.num_programs(1) - 1)
    def _():
        r = lax.broadcasted_iota(jnp.int32, (t_new, t_new), 0)
        col = lax.broadcasted_iota(jnp.int32, (t_new, t_new), 1)
        for h in range(N_HEADS):
            sl = slice(h * HEAD_DIM, (h + 1) * HEAD_DIM)
            s = _dot_nt(q_ref[:, sl], kn_ref[:, sl]) - cn_ref[h:h + 1, :t_new]
            update(h, jnp.where(col <= r, s, NEG_BIG), vn_ref[:, sl])
            o_ref[:, sl] = (acc_ref[h] / l_ref[h]).astype(o_ref.dtype)


def _sample_attn(q, cache_k, cache_v, kb, vb, c_cache, c_new, batch, t_new, past, chunk):
    assert past % chunk == 0
    nspec = pl.BlockSpec((t_new, D_MODEL), lambda b, s: (b, 0))
    cachespec = pl.BlockSpec((None, chunk * N_HEADS, HEAD_DIM), lambda b, s: (b, s, 0))
    ccspec = pl.BlockSpec((None, None, N_HEADS, chunk), lambda b, s: (b, s, 0, 0))
    cnspec = pl.BlockSpec((None, N_HEADS, LANES), lambda b, s: (b, 0, 0))
    head_rows = (batch, past * N_HEADS, HEAD_DIM)
    vmem = 2 * 2 * chunk * N_HEADS * HEAD_DIM * 4 + 8 * chunk * HEAD_DIM * 4 + (1 << 22)
    return pl.pallas_call(
        functools.partial(_sample_attn_kernel, chunk=chunk, t_new=t_new),
        grid=(batch, past // chunk),
        in_specs=[nspec, cachespec, cachespec, nspec, nspec, ccspec, cnspec],
        out_specs=nspec,
        out_shape=jax.ShapeDtypeStruct(q.shape, BF16),
        scratch_shapes=[pltpu.VMEM((N_HEADS, t_new, 1), F32), pltpu.VMEM((N_HEADS, t_new, 1), F32),
                        pltpu.VMEM((N_HEADS, t_new, HEAD_DIM), F32)],
        compiler_params=_params(("parallel", "arbitrary"), vmem),
        name="sample_attn",
    )(q, cache_k.reshape(head_rows), cache_v.reshape(head_rows), kb, vb, c_cache, c_new)


def _outproj_conv_kernel(x_ref, g_ref, o_ref, wa_ref, hist_ref, gn_ref, wb_ref, wc_ref, wu_ref, wg_ref, cw_ref,
                         wo_ref, y_ref, st_ref, tail_ref, *, tm):
    @pl.when(pl.program_id(1) == 0)
    def _():
        tail_ref[...] = jnp.zeros_like(tail_ref)
        tail_ref[8 - (CONV_K - 1):, :] = hist_ref[...]

    x, hb, cu = _conv_inputs(x_ref, g_ref, o_ref, wa_ref, gn_ref, wc_ref, wu_ref)
    tail = tail_ref[...]
    pos = lax.broadcasted_iota(jnp.int32, cu.shape, 0)
    y_ref[...] = _conv_outputs(x, hb, cu, pos, tail[6:7, :], tail[7:8, :], cw_ref, wg_ref, wb_ref, wo_ref)
    tail_ref[...] = cu[tm - 8:, :]
    st_ref[...] = cu[tm - (CONV_K - 1):, :]


def _conv_inputs(x_ref, g_ref, o_ref, wa_ref, gn_ref, wc_ref, wu_ref):
    gated = _silu(g_ref[...].astype(F32)) * o_ref[...].astype(F32)
    x = x_ref[...] + _dot(gated.astype(BF16), wa_ref[...])
    hb = (x * _rms_scale(x) * gn_ref[...]).astype(BF16)
    return x, hb, _dot(hb, wc_ref[...]) * _dot(hb, wu_ref[...])


def _conv_outputs(x, hb, cu, pos, h_m2, h_m1, cw_ref, wg_ref, wb_ref, wo_ref):
    s1 = jnp.where(pos == 0, h_m1, pltpu.roll(cu, 1, 0))
    s2 = jnp.where(pos == 0, h_m2, jnp.where(pos == 1, h_m1, pltpu.roll(cu, 2, 0)))
    cw = cw_ref[...]
    conv = cw[0:1, :] * s2 + cw[1:2, :] * s1 + cw[2:3, :] * cu
    mixed = _silu(_dot(hb, wg_ref[...])) * _dot(hb, wb_ref[...]) * conv
    return x + _dot(mixed.astype(BF16), wo_ref[...])


def _outproj_conv_short_kernel(x_ref, g_ref, o_ref, wa_ref, h0_ref, h1_ref, gn_ref, wb_ref, wc_ref, wu_ref, wg_ref,
                               cw_ref, wo_ref, y_ref, cu_ref, *, seq):
    x, hb, cu = _conv_inputs(x_ref, g_ref, o_ref, wa_ref, gn_ref, wc_ref, wu_ref)
    pos = lax.broadcasted_iota(jnp.int32, cu.shape, 0) & (seq - 1)
    y_ref[...] = _conv_outputs(x, hb, cu, pos, h0_ref[...], h1_ref[...], cw_ref, wg_ref, wb_ref, wo_ref)
    cu_ref[...] = cu


def _outproj_conv_short(x, gate, o, wa, hist, gn, wb, wc, wu, wg, cw, wo, batch, seq):
    n = batch * seq
    assert seq & (seq - 1) == 0 and seq >= CONV_K - 1 and n % 8 == 0
    h0 = jnp.repeat(hist[:, 0, :], seq, axis=0)
    h1 = jnp.repeat(hist[:, 1, :], seq, axis=0)
    args = (x, gate, o, wa, h0, h1, gn, wb, wc, wu, wg, cw, wo)
    vmem = 2 * 2 * 6 * D_MODEL * D_MODEL + 2 * n * D_MODEL * 24 + 10 * n * D_MODEL * 4
    y, cu = pl.pallas_call(
        functools.partial(_outproj_conv_short_kernel, seq=seq),
        out_shape=[jax.ShapeDtypeStruct(x.shape, F32), jax.ShapeDtypeStruct(x.shape, F32)],
        compiler_params=_params((), vmem),
        name="outproj_conv_short",
    )(*args)
    return y, cu.reshape(batch, seq, D_MODEL)[:, seq - (CONV_K - 1):, :]


def _outproj_conv(x, gate, o, wa, hist, gn, wb, wc, wu, wg, cw, wo, batch, seq, tm):
    assert seq % tm == 0 and tm % 8 == 0 and tm >= 8
    nt = seq // tm
    row = pl.BlockSpec((tm, D_MODEL), lambda b, t: (b * nt + t, 0))
    st = pl.BlockSpec((None, CONV_K - 1, D_MODEL), lambda b, t: (b, 0, 0))
    full = lambda a: pl.BlockSpec(a.shape, lambda b, t: (0, 0))
    w_bytes = 2 * 2 * 6 * D_MODEL * D_MODEL
    io_bytes = 2 * tm * D_MODEL * (4 + 2 + 2 + 4)
    tmp_bytes = 10 * tm * D_MODEL * 4
    return pl.pallas_call(
        functools.partial(_outproj_conv_kernel, tm=tm),
        grid=(batch, nt),
        in_specs=[row, row, row, full(wa), st, full(gn), full(wb), full(wc), full(wu), full(wg), full(cw),
                  full(wo)],
        out_specs=[row, st],
        out_shape=[jax.ShapeDtypeStruct(x.shape, F32),
                   jax.ShapeDtypeStruct((batch, CONV_K - 1, D_MODEL), F32)],
        scratch_shapes=[pltpu.VMEM((8, D_MODEL), F32)],
        compiler_params=_params(("parallel", "arbitrary"), w_bytes + io_bytes + tmp_bytes),
        name="outproj_conv",
    )(x, gate, o, wa, hist, gn, wb, wc, wu, wg, cw, wo)


def _row(v, width=None):
    v = v.astype(F32).reshape(1, -1)
    if width is not None and v.shape[1] < width:
        v = jnp.pad(v, ((0, 0), (0, width - v.shape[1])))
    return v


def _attn_weights(g_norm, w_in, b_f, g_q, g_k, w_out):
    aw = N_HEADS * HEAD_DIM
    wq, wk, wv, wg = (w_in[:, i * aw:(i + 1) * aw].astype(BF16) for i in range(4))
    wf = jnp.pad(w_in[:, 4 * aw:], ((0, 0), (0, LANES - N_HEADS))).astype(BF16)
    return (_row(g_norm), wq, wk, wv, wg, wf, _row(b_f, LANES), _row(g_q), _row(g_k)), w_out.astype(BF16)


def _heads_to_rows(lf, batch, length):
    return jnp.transpose(lf.reshape(batch, length, N_HEADS), (0, 2, 1)).reshape(batch * N_HEADS, length)


def _attn_prompt_layer(x, params, batch, seq, tm, tk):
    inw, wo = _attn_weights(*params)
    q, k, kb, v, vb, gate, lf = _attn_inproj(x, *inw, tm=tm)
    c = _cumsum_lanes(_heads_to_rows(lf, batch, seq))
    c = c.reshape(batch * N_HEADS, seq // tk, 1, tk)
    g_q, g_k = params[3], params[4]
    logit_bound = 1.01 * HEAD_DIM ** 0.5 * jnp.max(jnp.abs(g_q)) * jnp.max(jnp.abs(g_k))
    o = lax.cond(logit_bound < FAST_LOGIT_BOUND,
                 functools.partial(_flash_prompt, batch=batch, seq=seq, tk=tk, bounded=True),
                 functools.partial(_flash_prompt, batch=batch, seq=seq, tk=tk, bounded=False),
                 q, kb, vb, c)
    return (gate, o, wo), k, v, lf


def _attn_sample_layer(x, cache_k, cache_v, cache_logf, params, batch, t_new):
    past = cache_k.shape[1]
    inw, wo = _attn_weights(*params)
    q, k, kb, v, vb, gate, lf = _attn_inproj(x, *inw, tm=x.shape[0])
    lf_all = jnp.concatenate([cache_logf.astype(F32), lf.reshape(batch, t_new, N_HEADS)], axis=1)
    total = past + t_new
    padded = -(-total // LANES) * LANES
    lf_rows = _heads_to_rows(lf_all.reshape(batch * total, N_HEADS), batch, total)
    lf_rows = jnp.pad(lf_rows, ((0, 0), (0, padded - total)))
    c = _cumsum_lanes(lf_rows).reshape(batch, N_HEADS, padded)
    chunk = 2048
    c_cache = jnp.transpose(c[:, :, :past].reshape(batch, N_HEADS, past // chunk, chunk), (0, 2, 1, 3))
    c_new = c[:, :, past:past + LANES]
    o = _sample_attn(q, cache_k, cache_v, kb, vb, c_cache, c_new, batch, t_new, past, chunk)
    return (gate, o, wo), k, v, lf


def _conv_weights(g_norm, w_in, conv_w, w_out):
    w = D_MODEL
    wb, wc, wu, wg = (w_in[:, i * w:(i + 1) * w].astype(BF16) for i in range(4))
    return _row(g_norm), wb, wc, wu, wg, conv_w.astype(F32), w_out.astype(BF16)


def kernel(x_prompt, x_sample, cache_k_l0, cache_v_l0, cache_logf_l0, state_conv_l1, cache_k_l2, cache_v_l2, cache_logf_l2, state_conv_l3, norm_l0, w_in_l0, b_f_l0, qnorm_l0, knorm_l0, w_out_l0, norm_l1, w_in_l1, conv_w_l1, w_out_l1, norm_l2, w_in_l2, b_f_l2, qnorm_l2, knorm_l2, w_out_l2, norm_l3, w_in_l3, conv_w_l3, w_out_l3):
    batch, seq, d = x_prompt.shape
    dec_batch, dec_seq, _ = x_sample.shape
    caches = [(cache_k_l0, cache_v_l0, cache_logf_l0), (state_conv_l1,),
              (cache_k_l2, cache_v_l2, cache_logf_l2), (state_conv_l3,)]
    params = [(norm_l0, w_in_l0, b_f_l0, qnorm_l0, knorm_l0, w_out_l0),
              (norm_l1, w_in_l1, conv_w_l1, w_out_l1),
              (norm_l2, w_in_l2, b_f_l2, qnorm_l2, knorm_l2, w_out_l2),
              (norm_l3, w_in_l3, conv_w_l3, w_out_l3)]
    tm, tk = 512, 512
    yp = x_prompt.reshape(batch * seq, d)
    ys = x_sample.reshape(dec_batch * dec_seq, d)
    outs = []
    for i in range(0, len(params), 2):
        mix_p, kp, vp, lfp = _attn_prompt_layer(yp, params[i], batch, seq, tm, tk)
        mix_s, ks, vs, lfs = _attn_sample_layer(ys, *caches[i], params[i], dec_batch, dec_seq)
        outs += [kp.reshape(batch, seq, N_HEADS, HEAD_DIM), vp.reshape(batch, seq, N_HEADS, HEAD_DIM),
                 lfp.reshape(batch, seq, N_HEADS),
                 ks.reshape(dec_batch, dec_seq, N_HEADS, HEAD_DIM),
                 vs.reshape(dec_batch, dec_seq, N_HEADS, HEAD_DIM),
                 lfs.reshape(dec_batch, dec_seq, N_HEADS)]
        cw = _conv_weights(*params[i + 1])
        zero_hist = jnp.zeros((batch, CONV_K - 1, d), F32)
        yp, cp = _outproj_conv(yp, *mix_p, zero_hist, *cw, batch=batch, seq=seq, tm=tm)
        ys, cs = _outproj_conv_short(ys, *mix_s, caches[i + 1][0].astype(F32), *cw, batch=dec_batch, seq=dec_seq)
        outs += [cp, cs]
    return (yp.reshape(batch, seq, d), ys.reshape(dec_batch, dec_seq, d), *outs)
```

```python
import functools

import jax
import jax.numpy as jnp
from jax import lax
from jax.experimental import pallas as pl
from jax.experimental.pallas import tpu as pltpu

N_HEADS = 8
HEAD_DIM = 128
D_MODEL = 1024
CONV_K = 3
NORM_EPS = 1e-6
LOG2E = 1.4426950408889634
QK_SCALE_LOG2 = HEAD_DIM ** -0.5 * LOG2E
NEG_BIG = -1e30
FAST_LOGIT_BOUND = 60.0

LANES = 128
V7X_SCOPED_VMEM_BYTES = 60000 * 1024

F32 = jnp.float32
BF16 = jnp.bfloat16


def _params(semantics, vmem_bytes):
    return pltpu.CompilerParams(dimension_semantics=semantics,
                                vmem_limit_bytes=min(int(vmem_bytes), V7X_SCOPED_VMEM_BYTES))


def _rms_scale(x):
    return lax.rsqrt(jnp.mean(x * x, axis=-1, keepdims=True) + NORM_EPS)


def _dot(a, b):
    return jnp.dot(a, b, preferred_element_type=F32)


def _dot_nt(a, b):
    return lax.dot_general(a, b, (((1,), (1,)), ((), ())), preferred_element_type=F32)


def _silu(g):
    return g / (1.0 + jnp.exp(-g))


def _attn_inproj_kernel(x_ref, gn_ref, w_ref, wf_ref, bf_ref, gq_ref, gk_ref,
                        q_ref, k_ref, kb_ref, v_ref, vb_ref, g_ref, lf_ref):
    x = x_ref[...]
    hb = (x * _rms_scale(x) * gn_ref[...]).astype(BF16)

    tm = x.shape[0]
    zq = _dot(hb, w_ref[:, 0 * D_MODEL:1 * D_MODEL])
    zk = _dot(hb, w_ref[:, 1 * D_MODEL:2 * D_MODEL])
    zv = _dot(hb, w_ref[:, 2 * D_MODEL:3 * D_MODEL])
    gq = gq_ref[...] * QK_SCALE_LOG2
    gk = gk_ref[...]
    for h in range(N_HEADS):
        sl = slice(h * HEAD_DIM, (h + 1) * HEAD_DIM)
        rows = pl.ds(h, tm, stride=N_HEADS)
        qh = zq[:, sl]
        q_ref[:, sl] = (qh * _rms_scale(qh) * gq).astype(BF16)
        kh = zk[:, sl]
        kn = kh * _rms_scale(kh) * gk
        k_ref[rows, :] = kn
        kb_ref[:, sl] = kn.astype(BF16)
        v_ref[rows, :] = zv[:, sl]

    vb_ref[...] = zv.astype(BF16)
    g_ref[...] = _dot(hb, w_ref[:, 3 * D_MODEL:4 * D_MODEL]).astype(BF16)

    zf = _dot(hb, wf_ref[...]) + bf_ref[...]
    lf = jnp.minimum(zf, 0.0) - jnp.log1p(jnp.exp(-jnp.abs(zf)))
    lf_ref[...] = lf[:, :N_HEADS]


def _attn_inproj(x, gn, w, wf, bf, gq, gk, tm):
    n = x.shape[0]
    assert n % tm == 0
    row = lambda c: pl.BlockSpec((tm, c), lambda i: (i, 0))
    head_rows = pl.BlockSpec((tm * N_HEADS, HEAD_DIM), lambda i: (i, 0))
    full = lambda a: pl.BlockSpec(a.shape, lambda i: (0, 0))
    w_bytes = 2 * 2 * (4 * D_MODEL * D_MODEL + D_MODEL * LANES)
    io_bytes = 2 * tm * D_MODEL * (4 + 2 + 4 + 2 + 4 + 2 + 2)
    tmp_bytes = 6 * tm * D_MODEL * 4
    return pl.pallas_call(
        _attn_inproj_kernel,
        grid=(n // tm,),
        in_specs=[row(D_MODEL), full(gn), full(w), full(wf), full(bf), full(gq), full(gk)],
        out_specs=[row(D_MODEL), head_rows, row(D_MODEL), head_rows, row(D_MODEL), row(D_MODEL),
                   row(N_HEADS)],
        out_shape=[jax.ShapeDtypeStruct((n, D_MODEL), BF16),
                   jax.ShapeDtypeStruct((n * N_HEADS, HEAD_DIM), F32),
                   jax.ShapeDtypeStruct((n, D_MODEL), BF16),
                   jax.ShapeDtypeStruct((n * N_HEADS, HEAD_DIM), F32),
                   jax.ShapeDtypeStruct((n, D_MODEL), BF16),
                   jax.ShapeDtypeStruct((n, D_MODEL), BF16),
                   jax.ShapeDtypeStruct((n, N_HEADS), F32)],
        compiler_params=_params(("parallel",), w_bytes + io_bytes + tmp_bytes),
        name="attn_inproj",
    )(x, gn, w, wf, bf, gq, gk)


def _cumsum_kernel(x_ref, u_ref, o_ref):
    rows, length = x_ref.shape
    u = u_ref[...]
    carry = jnp.zeros((rows, 1), F32)
    for j in range(length // LANES):
        sl = slice(j * LANES, (j + 1) * LANES)
        x = x_ref[:, sl]
        hi = x.astype(BF16)
        r1 = x - hi.astype(F32)
        mid = r1.astype(BF16)
        lo = (r1 - mid.astype(F32)).astype(BF16)
        c = (_dot(hi, u) + _dot(mid, u)) + _dot(lo, u) + carry
        o_ref[:, sl] = c * LOG2E
        carry = c[:, LANES - 1:LANES]


def _cumsum_lanes(x):
    rows, length = x.shape
    assert length % LANES == 0 and rows % 8 == 0
    idx = jnp.arange(LANES)
    u = (idx[:, None] <= idx[None, :]).astype(BF16)
    return pl.pallas_call(
        _cumsum_kernel,
        out_shape=jax.ShapeDtypeStruct((rows, length), F32),
        compiler_params=_params((), 8 * rows * length * 4 + (1 << 22)),
        name="cumsum_lanes",
    )(x, u)


def _causal_mask(blk):
    r = lax.broadcasted_iota(jnp.int32, (blk, blk), 0)
    c = lax.broadcasted_iota(jnp.int32, (blk, blk), 1)
    return c <= r


def _flash_online_kernel(q_ref, k_ref, v_ref, c_ref, o_ref, *, blk):
    i = pl.program_id(2)
    q = q_ref[...]

    def step(j, carry, masked):
        m, l, acc = carry
        off = pl.multiple_of(j * blk, blk)
        kb = k_ref[pl.ds(off, blk), :]
        vb = v_ref[pl.ds(off, blk), :]
        s = _dot_nt(q, kb) - c_ref[j]
        if masked:
            s = jnp.where(_causal_mask(blk), s, NEG_BIG)
        m_new = jnp.maximum(m, jnp.max(s, axis=-1, keepdims=True))
        p = jnp.exp2(s - m_new)
        alpha = jnp.exp2(m - m_new)
        l = alpha * l + jnp.sum(p, axis=-1, keepdims=True)
        acc = alpha * acc + _dot(p.astype(BF16), vb)
        return m_new, l, acc

    init = (jnp.full((blk, 1), NEG_BIG, F32), jnp.zeros((blk, 1), F32), jnp.zeros((blk, HEAD_DIM), F32))
    carry = lax.fori_loop(0, i, lambda j, c: step(j, c, False), init)
    m, l, acc = step(i, carry, True)
    o_ref[...] = (acc / l).astype(o_ref.dtype)


VT_ROWS = HEAD_DIM + 16
FLASH_KEY_BLOCKS_PER_QUERY_BLOCK = 8


def _flash_bounded_kernel(q_ref, k_ref, v_ref, c_ref, o_ref, vt_ref, cs_ref, mask_ref, p_ref, acc_ref, *, tk, r):
    i = pl.program_id(2)
    tq = r * tk
    lane_tiles = tk // LANES

    @pl.when(i == 0)
    def _():
        sub = lax.broadcasted_iota(jnp.int32, (VT_ROWS - HEAD_DIM, tk), 0)
        ones_rows = jnp.where(sub == 0, 1.0, 0.0).astype(BF16)
        eye = (lax.broadcasted_iota(jnp.int32, (HEAD_DIM, HEAD_DIM), 0)
               == lax.broadcasted_iota(jnp.int32, (HEAD_DIM, HEAD_DIM), 1)).astype(BF16)
        for jb in range(v_ref.shape[0] // tk):
            vt_ref[jb, :HEAD_DIM, :] = _dot_nt(eye, v_ref[jb * tk:(jb + 1) * tk, :]).astype(BF16)
            vt_ref[jb, HEAD_DIM:, :] = ones_rows
            c_row = c_ref[jb]
            for n in range(lane_tiles):
                lanes = slice(n * LANES, (n + 1) * LANES)
                cs_ref[jb, lanes, :] = jnp.transpose(jnp.broadcast_to(c_row[:, lanes], (LANES, LANES)))
        key = lax.broadcasted_iota(jnp.int32, (tk, tk), 0)
        query = lax.broadcasted_iota(jnp.int32, (tk, tk), 1)
        mask_ref[0] = jnp.zeros((tk, tk), F32)
        mask_ref[1] = jnp.where(key <= query, 0.0, NEG_BIG)

    first = r * i
    c_t = jnp.concatenate([c_ref[first + a] for a in range(r)], axis=1)

    def keys(j):
        return k_ref[pl.ds(pl.multiple_of(j * tk, tk), tk), :]

    def logits_t(j, cols):
        c_s = jnp.concatenate([cs_ref[j]] * ((cols.stop - cols.start) // LANES), axis=1)
        return (_dot_nt(keys(j), q_ref[cols, :]) + c_t[:, cols]) - c_s

    def probs_t(slot, j, mask_index, first_part=0):
        for part in range(first_part, r):
            cols = slice(part * tk, (part + 1) * tk)
            z = logits_t(j, cols)
            if mask_index is not None and part == first_part:
                z = z + mask_ref[mask_index]
            p_ref[slot, :, cols] = jnp.exp2(z).astype(BF16)

    def accumulate(slot, j):
        for half in range(2):
            cols = slice(half * tq // 2, (half + 1) * tq // 2)
            acc_ref[:, cols] += _dot(vt_ref[j], p_ref[slot, :, cols])

    probs_t(0, 0, (i == 0).astype(jnp.int32))
    acc_ref[...] = jnp.zeros_like(acc_ref)

    def two_blocks(j):
        accumulate(0, j)
        probs_t(1, j + 1, None)
        accumulate(1, j + 1)
        probs_t(0, j + 2, (j + 2 == first).astype(jnp.int32))

    def trip(t, _):
        two_blocks(4 * t)
        two_blocks(4 * t + 2)
        return 0

    lax.fori_loop(0, first // 4, trip, 0)
    accumulate(0, first)
    for d in range(1, r):
        slot = d % 2
        probs_t(slot, first + d, 1, first_part=d)
        acc_ref[:, d * tk:] += _dot(vt_ref[first + d], p_ref[slot, :, d * tk:])
    acc = acc_ref[...]
    o_ref[...] = jnp.transpose(acc[:HEAD_DIM, :] / acc[HEAD_DIM:HEAD_DIM + 1, :]).astype(o_ref.dtype)


def _flash_prompt(q, kb, vb, c, batch, seq, tk, bounded):
    r = FLASH_KEY_BLOCKS_PER_QUERY_BLOCK if bounded else 1
    tq = r * tk
    nq = seq // tq
    assert seq % tq == 0 and (r == 1 or r % 4 == 0)
    qspec = pl.BlockSpec((tq, HEAD_DIM), lambda b, h, i: (b * nq + i, h))
    kvspec = pl.BlockSpec((seq, HEAD_DIM), lambda b, h, i: (b, h))
    cspec = pl.BlockSpec((None, seq // tk, 1, tk), lambda b, h, i: (b * N_HEADS + h, 0, 0, 0))
    vmem = 2 * 2 * seq * HEAD_DIM * 2 + 4 * tq * HEAD_DIM * 2 + 2 * seq * 4 + 8 * tq * tk * 4
    if bounded:
        body = functools.partial(_flash_bounded_kernel, tk=tk, r=r)
        scratch = [pltpu.VMEM((seq // tk, VT_ROWS, tk), BF16),
                   pltpu.VMEM((seq // tk, tk, LANES), F32),
                   pltpu.VMEM((2, tk, tk), F32),
                   pltpu.VMEM((2, tk, tq), BF16),
                   pltpu.VMEM((VT_ROWS, tq), F32)]
        vmem += seq * VT_ROWS * 2 + seq * LANES * 4 + 2 * tk * tk * 4 + 2 * tq * tk * 2 + tq * VT_ROWS * 4
    else:
        body = functools.partial(_flash_online_kernel, blk=tk)
        scratch = []
    return pl.pallas_call(
        body,
        grid=(batch, N_HEADS, nq),
        in_specs=[qspec, kvspec, kvspec, cspec],
        out_specs=qspec,
        out_shape=jax.ShapeDtypeStruct(q.shape, BF16),
        scratch_shapes=scratch,
        compiler_params=_params(("parallel", "parallel", "arbitrary"), vmem),
        name="flash_bounded" if bounded else "flash_online",
    )(q, kb, vb, c)


def _sample_attn_kernel(q_ref, ck_ref, cv_ref, kn_ref, vn_ref, cc_ref, cn_ref, o_ref, m_ref, l_ref, acc_ref,
                        *, chunk, t_new):
    step = pl.program_id(1)

    @pl.when(step == 0)
    def _():
        m_ref[...] = jnp.full_like(m_ref, NEG_BIG)
        l_ref[...] = jnp.zeros_like(l_ref)
        acc_ref[...] = jnp.zeros_like(acc_ref)

    def rescale(h, s):
        m_old = m_ref[h]
        m_new = jnp.maximum(m_old, jnp.max(s, axis=-1, keepdims=True))
        p = jnp.exp2(s - m_new)
        alpha = jnp.exp2(m_old - m_new)
        l_ref[h] = alpha * l_ref[h] + jnp.sum(p, axis=-1, keepdims=True)
        m_ref[h] = m_new
        return alpha, p.astype(BF16)

    def update(h, s, values):
        alpha, p = rescale(h, s)
        acc_ref[h] = alpha * acc_ref[h] + _dot(p, values)

    heads = range(N_HEADS)
    head_rows = [pl.ds(h, chunk, stride=N_HEADS) for h in heads]
    scores = [_dot_nt(q_ref[:, h * HEAD_DIM:(h + 1) * HEAD_DIM], ck_ref[head_rows[h], :].astype(BF16))
              - cc_ref[h:h + 1, :] for h in heads]
    scaled = [rescale(h, scores[h]) for h in heads]
    for h in heads:
        alpha, p = scaled[h]
        acc_ref[h] = alpha * acc_ref[h] + _dot(p, cv_ref[head_rows[h], :].astype(BF16))

    @pl.when(step == pl.num_programs(1) - 1)
    def _():
        r = lax.broadcasted_iota(jnp.int32, (t_new, t_new), 0)
        col = lax.broadcasted_iota(jnp.int32, (t_new, t_new), 1)
        for h in range(N_HEADS):
            sl = slice(h * HEAD_DIM, (h + 1) * HEAD_DIM)
            s = _dot_nt(q_ref[:, sl], kn_ref[:, sl]) - cn_ref[h:h + 1, :t_new]
            update(h, jnp.where(col <= r, s, NEG_BIG), vn_ref[:, sl])
            o_ref[:, sl] = (acc_ref[h] / l_ref[h]).astype(o_ref.dtype)


def _sample_attn(q, cache_k, cache_v, kb, vb, c_cache, c_new, batch, t_new, past, chunk):
    assert past % chunk == 0
    nspec = pl.BlockSpec((t_new, D_MODEL), lambda b, s: (b, 0))
    cachespec = pl.BlockSpec((None, chunk * N_HEADS, HEAD_DIM), lambda b, s: (b, s, 0))
    ccspec = pl.BlockSpec((None, None, N_HEADS, chunk), lambda b, s: (b, s, 0, 0))
    cnspec = pl.BlockSpec((None, N_HEADS, LANES), lambda b, s: (b, 0, 0))
    head_rows = (batch, past * N_HEADS, HEAD_DIM)
    vmem = 2 * 2 * chunk * N_HEADS * HEAD_DIM * 4 + 8 * chunk * HEAD_DIM * 4 + (1 << 22)
    return pl.pallas_call(
        functools.partial(_sample_attn_kernel, chunk=chunk, t_new=t_new),
        grid=(batch, past // chunk),
        in_specs=[nspec, cachespec, cachespec, nspec, nspec, ccspec, cnspec],
        out_specs=nspec,
        out_shape=jax.ShapeDtypeStruct(q.shape, BF16),
        scratch_shapes=[pltpu.VMEM((N_HEADS, t_new, 1), F32), pltpu.VMEM((N_HEADS, t_new, 1), F32),
                        pltpu.VMEM((N_HEADS, t_new, HEAD_DIM), F32)],
        compiler_params=_params(("parallel", "arbitrary"), vmem),
        name="sample_attn",
    )(q, cache_k.reshape(head_rows), cache_v.reshape(head_rows), kb, vb, c_cache, c_new)


def _outproj_conv_kernel(x_ref, g_ref, o_ref, wa_ref, hist_ref, gn_ref, wb_ref, wc_ref, wu_ref, wg_ref, cw_ref,
                         wo_ref, y_ref, st_ref, tail_ref, *, tm):
    @pl.when(pl.program_id(1) == 0)
    def _():
        tail_ref[...] = jnp.zeros_like(tail_ref)
        tail_ref[8 - (CONV_K - 1):, :] = hist_ref[...]

    x, hb, cu = _conv_inputs(x_ref, g_ref, o_ref, wa_ref, gn_ref, wc_ref, wu_ref)
    tail = tail_ref[...]
    pos = lax.broadcasted_iota(jnp.int32, cu.shape, 0)
    y_ref[...] = _conv_outputs(x, hb, cu, pos, tail[6:7, :], tail[7:8, :], cw_ref, wg_ref, wb_ref, wo_ref)
    tail_ref[...] = cu[tm - 8:, :]
    st_ref[...] = cu[tm - (CONV_K - 1):, :]


def _conv_inputs(x_ref, g_ref, o_ref, wa_ref, gn_ref, wc_ref, wu_ref):
    gated = _silu(g_ref[...].astype(F32)) * o_ref[...].astype(F32)
    x = x_ref[...] + _dot(gated.astype(BF16), wa_ref[...])
    hb = (x * _rms_scale(x) * gn_ref[...]).astype(BF16)
    return x, hb, _dot(hb, wc_ref[...]) * _dot(hb, wu_ref[...])


def _conv_outputs(x, hb, cu, pos, h_m2, h_m1, cw_ref, wg_ref, wb_ref, wo_ref):
    s1 = jnp.where(pos == 0, h_m1, pltpu.roll(cu, 1, 0))
    s2 = jnp.where(pos == 0, h_m2, jnp.where(pos == 1, h_m1, pltpu.roll(cu, 2, 0)))
    cw = cw_ref[...]
    conv = cw[0:1, :] * s2 + cw[1:2, :] * s1 + cw[2:3, :] * cu
    mixed = _silu(_dot(hb, wg_ref[...])) * _dot(hb, wb_ref[...]) * conv
    return x + _dot(mixed.astype(BF16), wo_ref[...])


def _outproj_conv_short_kernel(x_ref, g_ref, o_ref, wa_ref, h0_ref, h1_ref, gn_ref, wb_ref, wc_ref, wu_ref, wg_ref,
                               cw_ref, wo_ref, y_ref, cu_ref, *, seq):
    x, hb, cu = _conv_inputs(x_ref, g_ref, o_ref, wa_ref, gn_ref, wc_ref, wu_ref)
    pos = lax.broadcasted_iota(jnp.int32, cu.shape, 0) & (seq - 1)
    y_ref[...] = _conv_outputs(x, hb, cu, pos, h0_ref[...], h1_ref[...], cw_ref, wg_ref, wb_ref, wo_ref)
    cu_ref[...] = cu


def _outproj_conv_short(x, gate, o, wa, hist, gn, wb, wc, wu, wg, cw, wo, batch, seq):
    n = batch * seq
    assert seq & (seq - 1) == 0 and seq >= CONV_K - 1 and n % 8 == 0
    h0 = jnp.repeat(hist[:, 0, :], seq, axis=0)
    h1 = jnp.repeat(hist[:, 1, :], seq, axis=0)
    args = (x, gate, o, wa, h0, h1, gn, wb, wc, wu, wg, cw, wo)
    vmem = 2 * 2 * 6 * D_MODEL * D_MODEL + 2 * n * D_MODEL * 24 + 10 * n * D_MODEL * 4
    y, cu = pl.pallas_call(
        functools.partial(_outproj_conv_short_kernel, seq=seq),
        out_shape=[jax.ShapeDtypeStruct(x.shape, F32), jax.ShapeDtypeStruct(x.shape, F32)],
        compiler_params=_params((), vmem),
        name="outproj_conv_short",
    )(*args)
    return y, cu.reshape(batch, seq, D_MODEL)[:, seq - (CONV_K - 1):, :]


def _outproj_conv(x, gate, o, wa, hist, gn, wb, wc, wu, wg, cw, wo, batch, seq, tm):
    assert seq % tm == 0 and tm % 8 == 0 and tm >= 8
    nt = seq // tm
    row = pl.BlockSpec((tm, D_MODEL), lambda b, t: (b * nt + t, 0))
    st = pl.BlockSpec((None, CONV_K - 1, D_MODEL), lambda b, t: (b, 0, 0))
    full = lambda a: pl.BlockSpec(a.shape, lambda b, t: (0, 0))
    w_bytes = 2 * 2 * 6 * D_MODEL * D_MODEL
    io_bytes = 2 * tm * D_MODEL * (4 + 2 + 2 + 4)
    tmp_bytes = 10 * tm * D_MODEL * 4
    return pl.pallas_call(
        functools.partial(_outproj_conv_kernel, tm=tm),
        grid=(batch, nt),
        in_specs=[row, row, row, full(wa), st, full(gn), full(wb), full(wc), full(wu), full(wg), full(cw),
                  full(wo)],
        out_specs=[row, st],
        out_shape=[jax.ShapeDtypeStruct(x.shape, F32),
                   jax.ShapeDtypeStruct((batch, CONV_K - 1, D_MODEL), F32)],
        scratch_shapes=[pltpu.VMEM((8, D_MODEL), F32)],
        compiler_params=_params(("parallel", "arbitrary"), w_bytes + io_bytes + tmp_bytes),
        name="outproj_conv",
    )(x, gate, o, wa, hist, gn, wb, wc, wu, wg, cw, wo)


def _row(v, width=None):
    v = v.astype(F32).reshape(1, -1)
    if width is not None and v.shape[1] < width:
        v = jnp.pad(v, ((0, 0), (0, width - v.shape[1])))
    return v


def _attn_weights(g_norm, w_in, b_f, g_q, g_k, w_out):
    aw = N_HEADS * HEAD_DIM
    assert aw == D_MODEL
    w = w_in[:, :4 * aw].astype(BF16)
    wf = jnp.pad(w_in[:, 4 * aw:], ((0, 0), (0, LANES - N_HEADS))).astype(BF16)
    return (_row(g_norm), w, wf, _row(b_f, LANES), _row(g_q), _row(g_k)), w_out.astype(BF16)


def _heads_to_rows(lf, batch, length):
    return jnp.transpose(lf.reshape(batch, length, N_HEADS), (0, 2, 1)).reshape(batch * N_HEADS, length)


def _attn_prompt_layer(x, params, batch, seq, tm, tk):
    inw, wo = _attn_weights(*params)
    q, k, kb, v, vb, gate, lf = _attn_inproj(x, *inw, tm=tm)
    c = _cumsum_lanes(_heads_to_rows(lf, batch, seq))
    c = c.reshape(batch * N_HEADS, seq // tk, 1, tk)
    g_q, g_k = params[3], params[4]
    logit_bound = 1.01 * HEAD_DIM ** 0.5 * jnp.max(jnp.abs(g_q)) * jnp.max(jnp.abs(g_k))
    o = lax.cond(logit_bound < FAST_LOGIT_BOUND,
                 functools.partial(_flash_prompt, batch=batch, seq=seq, tk=tk, bounded=True),
                 functools.partial(_flash_prompt, batch=batch, seq=seq, tk=tk, bounded=False),
                 q, kb, vb, c)
    return (gate, o, wo), k, v, lf


def _attn_sample_layer(x, cache_k, cache_v, cache_logf, params, batch, t_new):
    past = cache_k.shape[1]
    inw, wo = _attn_weights(*params)
    q, k, kb, v, vb, gate, lf = _attn_inproj(x, *inw, tm=x.shape[0])
    lf_all = jnp.concatenate([cache_logf.astype(F32), lf.reshape(batch, t_new, N_HEADS)], axis=1)
    total = past + t_new
    padded = -(-total // LANES) * LANES
    lf_rows = _heads_to_rows(lf_all.reshape(batch * total, N_HEADS), batch, total)
    lf_rows = jnp.pad(lf_rows, ((0, 0), (0, padded - total)))
    c = _cumsum_lanes(lf_rows).reshape(batch, N_HEADS, padded)
    chunk = 2048
    c_cache = jnp.transpose(c[:, :, :past].reshape(batch, N_HEADS, past // chunk, chunk), (0, 2, 1, 3))
    c_new = c[:, :, past:past + LANES]
    o = _sample_attn(q, cache_k, cache_v, kb, vb, c_cache, c_new, batch, t_new, past, chunk)
    return (gate, o, wo), k, v, lf


def _conv_weights(g_norm, w_in, conv_w, w_out):
    w = D_MODEL
    wb, wc, wu, wg = (w_in[:, i * w:(i + 1) * w].astype(BF16) for i in range(4))
    return _row(g_norm), wb, wc, wu, wg, conv_w.astype(F32), w_out.astype(BF16)


def kernel(x_prompt, x_sample, cache_k_l0, cache_v_l0, cache_logf_l0, state_conv_l1, cache_k_l2, cache_v_l2, cache_logf_l2, state_conv_l3, norm_l0, w_in_l0, b_f_l0, qnorm_l0, knorm_l0, w_out_l0, norm_l1, w_in_l1, conv_w_l1, w_out_l1, norm_l2, w_in_l2, b_f_l2, qnorm_l2, knorm_l2, w_out_l2, norm_l3, w_in_l3, conv_w_l3, w_out_l3):
    batch, seq, d = x_prompt.shape
    dec_batch, dec_seq, _ = x_sample.shape
    caches = [(cache_k_l0, cache_v_l0, cache_logf_l0), (state_conv_l1,),
              (cache_k_l2, cache_v_l2, cache_logf_l2), (state_conv_l3,)]
    params = [(norm_l0, w_in_l0, b_f_l0, qnorm_l0, knorm_l0, w_out_l0),
              (norm_l1, w_in_l1, conv_w_l1, w_out_l1),
              (norm_l2, w_in_l2, b_f_l2, qnorm_l2, knorm_l2, w_out_l2),
              (norm_l3, w_in_l3, conv_w_l3, w_out_l3)]
    tm, tk = 512, 512
    yp = x_prompt.reshape(batch * seq, d)
    ys = x_sample.reshape(dec_batch * dec_seq, d)
    outs = []
    for i in range(0, len(params), 2):
        mix_p, kp, vp, lfp = _attn_prompt_layer(yp, params[i], batch, seq, tm, tk)
        mix_s, ks, vs, lfs = _attn_sample_layer(ys, *caches[i], params[i], dec_batch, dec_seq)
        outs += [kp.reshape(batch, seq, N_HEADS, HEAD_DIM), vp.reshape(batch, seq, N_HEADS, HEAD_DIM),
                 lfp.reshape(batch, seq, N_HEADS),
                 ks.reshape(dec_batch, dec_seq, N_HEADS, HEAD_DIM),
                 vs.reshape(dec_batch, dec_seq, N_HEADS, HEAD_DIM),
                 lfs.reshape(dec_batch, dec_seq, N_HEADS)]
        cw = _conv_weights(*params[i + 1])
        zero_hist = jnp.zeros((batch, CONV_K - 1, d), F32)
        yp, cp = _outproj_conv(yp, *mix_p, zero_hist, *cw, batch=batch, seq=seq, tm=tm)
        ys, cs = _outproj_conv_short(ys, *mix_s, caches[i + 1][0].astype(F32), *cw, batch=dec_batch, seq=dec_seq)
        outs += [cp, cs]
    return (yp.reshape(batch, seq, d), ys.reshape(dec_batch, dec_seq, d), *outs)
```

```python
import functools

import jax
import jax.numpy as jnp
from jax import lax
from jax.experimental import pallas as pl
from jax.experimental.pallas import tpu as pltpu

N_HEADS = 8
HEAD_DIM = 128
D_MODEL = 1024
CONV_K = 3
NORM_EPS = 1e-6
LOG2E = 1.4426950408889634
QK_SCALE_LOG2 = HEAD_DIM ** -0.5 * LOG2E
NEG_BIG = -1e30
FAST_LOGIT_BOUND = 60.0

LANES = 128
V7X_SCOPED_VMEM_BYTES = 60000 * 1024

F32 = jnp.float32
BF16 = jnp.bfloat16


def _params(semantics, vmem_bytes):
    return pltpu.CompilerParams(dimension_semantics=semantics,
                                vmem_limit_bytes=min(int(vmem_bytes), V7X_SCOPED_VMEM_BYTES))


def _rms_scale(x):
    return lax.rsqrt(jnp.mean(x * x, axis=-1, keepdims=True) + NORM_EPS)


def _dot(a, b):
    return jnp.dot(a, b, preferred_element_type=F32)


def _dot_nt(a, b):
    return lax.dot_general(a, b, (((1,), (1,)), ((), ())), preferred_element_type=F32)


def _silu(g):
    return g / (1.0 + jnp.exp(-g))


def _attn_inproj_kernel(x_ref, gn_ref, wq_ref, wk_ref, wv_ref, wg_ref, wf_ref, bf_ref, gq_ref, gk_ref,
                        q_ref, k_ref, kb_ref, v_ref, vb_ref, g_ref, lf_ref):
    x = x_ref[...]
    hb = (x * _rms_scale(x) * gn_ref[...]).astype(BF16)

    tm = x.shape[0]
    zq = _dot(hb, wq_ref[...])
    zk = _dot(hb, wk_ref[...])
    zv = _dot(hb, wv_ref[...])
    gq = gq_ref[...] * QK_SCALE_LOG2
    gk = gk_ref[...]
    for h in range(N_HEADS):
        sl = slice(h * HEAD_DIM, (h + 1) * HEAD_DIM)
        rows = pl.ds(h, tm, stride=N_HEADS)
        qh = zq[:, sl]
        q_ref[:, sl] = (qh * _rms_scale(qh) * gq).astype(BF16)
        kh = zk[:, sl]
        kn = kh * _rms_scale(kh) * gk
        k_ref[rows, :] = kn
        kb_ref[:, sl] = kn.astype(BF16)
        v_ref[rows, :] = zv[:, sl]

    vb_ref[...] = zv.astype(BF16)
    g_ref[...] = _dot(hb, wg_ref[...]).astype(BF16)

    zf = _dot(hb, wf_ref[...]) + bf_ref[...]
    lf = jnp.minimum(zf, 0.0) - jnp.log1p(jnp.exp(-jnp.abs(zf)))
    lf_ref[...] = lf[:, :N_HEADS]


def _attn_inproj(x, gn, wq, wk, wv, wg, wf, bf, gq, gk, tm):
    n = x.shape[0]
    assert n % tm == 0
    row = lambda c: pl.BlockSpec((tm, c), lambda i: (i, 0))
    head_rows = pl.BlockSpec((tm * N_HEADS, HEAD_DIM), lambda i: (i, 0))
    full = lambda a: pl.BlockSpec(a.shape, lambda i: (0, 0))
    w_bytes = 2 * 2 * (4 * D_MODEL * D_MODEL + D_MODEL * LANES)
    io_bytes = 2 * tm * D_MODEL * (4 + 2 + 4 + 2 + 4 + 2 + 2)
    tmp_bytes = 6 * tm * D_MODEL * 4
    return pl.pallas_call(
        _attn_inproj_kernel,
        grid=(n // tm,),
        in_specs=[row(D_MODEL), full(gn), full(wq), full(wk), full(wv), full(wg), full(wf), full(bf),
                  full(gq), full(gk)],
        out_specs=[row(D_MODEL), head_rows, row(D_MODEL), head_rows, row(D_MODEL), row(D_MODEL),
                   row(N_HEADS)],
        out_shape=[jax.ShapeDtypeStruct((n, D_MODEL), BF16),
                   jax.ShapeDtypeStruct((n * N_HEADS, HEAD_DIM), F32),
                   jax.ShapeDtypeStruct((n, D_MODEL), BF16),
                   jax.ShapeDtypeStruct((n * N_HEADS, HEAD_DIM), F32),
                   jax.ShapeDtypeStruct((n, D_MODEL), BF16),
                   jax.ShapeDtypeStruct((n, D_MODEL), BF16),
                   jax.ShapeDtypeStruct((n, N_HEADS), F32)],
        compiler_params=_params(("parallel",), w_bytes + io_bytes + tmp_bytes),
        name="attn_inproj",
    )(x, gn, wq, wk, wv, wg, wf, bf, gq, gk)


def _cumsum_kernel(x_ref, u_ref, o_ref):
    rows, length = x_ref.shape
    u = u_ref[...]
    carry = jnp.zeros((rows, 1), F32)
    for j in range(length // LANES):
        sl = slice(j * LANES, (j + 1) * LANES)
        x = x_ref[:, sl]
        hi = x.astype(BF16)
        r1 = x - hi.astype(F32)
        mid = r1.astype(BF16)
        lo = (r1 - mid.astype(F32)).astype(BF16)
        c = (_dot(hi, u) + _dot(mid, u)) + _dot(lo, u) + carry
        o_ref[:, sl] = c * LOG2E
        carry = c[:, LANES - 1:LANES]


def _cumsum_lanes(x):
    rows, length = x.shape
    assert length % LANES == 0 and rows % 8 == 0
    idx = jnp.arange(LANES)
    u = (idx[:, None] <= idx[None, :]).astype(BF16)
    return pl.pallas_call(
        _cumsum_kernel,
        out_shape=jax.ShapeDtypeStruct((rows, length), F32),
        compiler_params=_params((), 8 * rows * length * 4 + (1 << 22)),
        name="cumsum_lanes",
    )(x, u)


def _causal_mask(blk):
    r = lax.broadcasted_iota(jnp.int32, (blk, blk), 0)
    c = lax.broadcasted_iota(jnp.int32, (blk, blk), 1)
    return c <= r


def _flash_online_kernel(q_ref, k_ref, v_ref, c_ref, o_ref, *, blk):
    i = pl.program_id(2)
    q = q_ref[...]

    def step(j, carry, masked):
        m, l, acc = carry
        off = pl.multiple_of(j * blk, blk)
        kb = k_ref[pl.ds(off, blk), :]
        vb = v_ref[pl.ds(off, blk), :]
        s = _dot_nt(q, kb) - c_ref[j]
        if masked:
            s = jnp.where(_causal_mask(blk), s, NEG_BIG)
        m_new = jnp.maximum(m, jnp.max(s, axis=-1, keepdims=True))
        p = jnp.exp2(s - m_new)
        alpha = jnp.exp2(m - m_new)
        l = alpha * l + jnp.sum(p, axis=-1, keepdims=True)
        acc = alpha * acc + _dot(p.astype(BF16), vb)
        return m_new, l, acc

    init = (jnp.full((blk, 1), NEG_BIG, F32), jnp.zeros((blk, 1), F32), jnp.zeros((blk, HEAD_DIM), F32))
    carry = lax.fori_loop(0, i, lambda j, c: step(j, c, False), init)
    m, l, acc = step(i, carry, True)
    o_ref[...] = (acc / l).astype(o_ref.dtype)


VT_ROWS = HEAD_DIM + 16
FLASH_KEY_BLOCKS_PER_QUERY_BLOCK = 8


def _flash_bounded_kernel(q_ref, k_ref, v_ref, c_ref, o_ref, vt_ref, cs_ref, mask_ref, p_ref, acc_ref, *, tk, r):
    i = pl.program_id(2)
    tq = r * tk
    lane_tiles = tk // LANES

    @pl.when(i == 0)
    def _():
        sub = lax.broadcasted_iota(jnp.int32, (VT_ROWS - HEAD_DIM, tk), 0)
        ones_rows = jnp.where(sub == 0, 1.0, 0.0).astype(BF16)
        eye = (lax.broadcasted_iota(jnp.int32, (HEAD_DIM, HEAD_DIM), 0)
               == lax.broadcasted_iota(jnp.int32, (HEAD_DIM, HEAD_DIM), 1)).astype(BF16)
        for jb in range(v_ref.shape[0] // tk):
            vt_ref[jb, :HEAD_DIM, :] = _dot_nt(eye, v_ref[jb * tk:(jb + 1) * tk, :]).astype(BF16)
            vt_ref[jb, HEAD_DIM:, :] = ones_rows
            c_row = c_ref[jb]
            for n in range(lane_tiles):
                lanes = slice(n * LANES, (n + 1) * LANES)
                cs_ref[jb, lanes, :] = jnp.transpose(jnp.broadcast_to(c_row[:, lanes], (LANES, LANES)))
        key = lax.broadcasted_iota(jnp.int32, (tk, tk), 0)
        query = lax.broadcasted_iota(jnp.int32, (tk, tk), 1)
        mask_ref[0] = jnp.zeros((tk, tk), F32)
        mask_ref[1] = jnp.where(key <= query, 0.0, NEG_BIG)

    first = r * i
    c_t = jnp.concatenate([c_ref[first + a] for a in range(r)], axis=1)

    def keys(j):
        return k_ref[pl.ds(pl.multiple_of(j * tk, tk), tk), :]

    def logits_t(j, cols):
        c_s = jnp.concatenate([cs_ref[j]] * ((cols.stop - cols.start) // LANES), axis=1)
        return (_dot_nt(keys(j), q_ref[cols, :]) + c_t[:, cols]) - c_s

    def probs_t(slot, j, mask_index, first_part=0):
        for part in range(first_part, r):
            cols = slice(part * tk, (part + 1) * tk)
            z = logits_t(j, cols)
            if mask_index is not None and part == first_part:
                z = z + mask_ref[mask_index]
            p_ref[slot, :, cols] = jnp.exp2(z).astype(BF16)

    def accumulate(slot, j):
        for half in range(2):
            cols = slice(half * tq // 2, (half + 1) * tq // 2)
            acc_ref[:, cols] += _dot(vt_ref[j], p_ref[slot, :, cols])

    probs_t(0, 0, (i == 0).astype(jnp.int32))
    acc_ref[...] = jnp.zeros_like(acc_ref)

    def two_blocks(j):
        accumulate(0, j)
        probs_t(1, j + 1, None)
        accumulate(1, j + 1)
        probs_t(0, j + 2, (j + 2 == first).astype(jnp.int32))

    def trip(t, _):
        two_blocks(4 * t)
        two_blocks(4 * t + 2)
        return 0

    lax.fori_loop(0, first // 4, trip, 0)
    accumulate(0, first)
    for d in range(1, r):
        slot = d % 2
        probs_t(slot, first + d, 1, first_part=d)
        acc_ref[:, d * tk:] += _dot(vt_ref[first + d], p_ref[slot, :, d * tk:])
    acc = acc_ref[...]
    o_ref[...] = jnp.transpose(acc[:HEAD_DIM, :] / acc[HEAD_DIM:HEAD_DIM + 1, :]).astype(o_ref.dtype)


def _flash_prompt(q, kb, vb, c, batch, seq, tk, bounded):
    r = FLASH_KEY_BLOCKS_PER_QUERY_BLOCK if bounded else 1
    tq = r * tk
    nq = seq // tq
    assert seq % tq == 0 and (r == 1 or r % 4 == 0)
    qspec = pl.BlockSpec((tq, HEAD_DIM), lambda b, h, i: (b * nq + i, h))
    kvspec = pl.BlockSpec((seq, HEAD_DIM), lambda b, h, i: (b, h))
    cspec = pl.BlockSpec((None, seq // tk, 1, tk), lambda b, h, i: (b * N_HEADS + h, 0, 0, 0))
    vmem = 2 * 2 * seq * HEAD_DIM * 2 + 4 * tq * HEAD_DIM * 2 + 2 * seq * 4 + 8 * tq * tk * 4
    if bounded:
        body = functools.partial(_flash_bounded_kernel, tk=tk, r=r)
        scratch = [pltpu.VMEM((seq // tk, VT_ROWS, tk), BF16),
                   pltpu.VMEM((seq // tk, tk, LANES), F32),
                   pltpu.VMEM((2, tk, tk), F32),
                   pltpu.VMEM((2, tk, tq), BF16),
                   pltpu.VMEM((VT_ROWS, tq), F32)]
        vmem += seq * VT_ROWS * 2 + seq * LANES * 4 + 2 * tk * tk * 4 + 2 * tq * tk * 2 + tq * VT_ROWS * 4
    else:
        body = functools.partial(_flash_online_kernel, blk=tk)
        scratch = []
    return pl.pallas_call(
        body,
        grid=(batch, N_HEADS, nq),
        in_specs=[qspec, kvspec, kvspec, cspec],
        out_specs=qspec,
        out_shape=jax.ShapeDtypeStruct(q.shape, BF16),
        scratch_shapes=scratch,
        compiler_params=_params(("parallel", "parallel", "arbitrary"), vmem),
        name="flash_bounded" if bounded else "flash_online",
    )(q, kb, vb, c)


def _sample_attn_kernel(q_ref, ck_ref, cv_ref, kn_ref, vn_ref, cc_ref, cn_ref, o_ref, m_ref, l_ref, acc_ref,
                        *, chunk, t_new):
    step = pl.program_id(1)

    @pl.when(step == 0)
    def _():
        m_ref[...] = jnp.full_like(m_ref, NEG_BIG)
        l_ref[...] = jnp.zeros_like(l_ref)
        acc_ref[...] = jnp.zeros_like(acc_ref)

    def rescale(h, s):
        m_old = m_ref[h]
        m_new = jnp.maximum(m_old, jnp.max(s, axis=-1, keepdims=True))
        p = jnp.exp2(s - m_new)
        alpha = jnp.exp2(m_old - m_new)
        l_ref[h] = alpha * l_ref[h] + jnp.sum(p, axis=-1, keepdims=True)
        m_ref[h] = m_new
        return alpha, p.astype(BF16)

    heads = range(N_HEADS)
    head_rows = [pl.ds(h, chunk, stride=N_HEADS) for h in heads]
    scores = [_dot_nt(q_ref[:, h * HEAD_DIM:(h + 1) * HEAD_DIM], ck_ref[head_rows[h], :].astype(BF16))
              - cc_ref[h:h + 1, :] for h in heads]
    scaled = [rescale(h, scores[h]) for h in heads]
    for h in heads:
        alpha, p = scaled[h]
        acc_ref[h] = alpha * acc_ref[h] + _dot(p, cv_ref[head_rows[h], :].astype(BF16))

    @pl.when(step == pl.num_programs(1) - 1)
    def _():
        r = lax.broadcasted_iota(jnp.int32, (t_new, t_new), 0)
        col = lax.broadcasted_iota(jnp.int32, (t_new, t_new), 1)
        cols = [slice(h * HEAD_DIM, (h + 1) * HEAD_DIM) for h in heads]
        new_scores = [jnp.where(col <= r, _dot_nt(q_ref[:, cols[h]], kn_ref[:, cols[h]]) - cn_ref[h:h + 1, :t_new],
                                NEG_BIG) for h in heads]
        new_scaled = [rescale(h, new_scores[h]) for h in heads]
        for h in heads:
            alpha, p = new_scaled[h]
            acc = alpha * acc_ref[h] + _dot(p, vn_ref[:, cols[h]])
            o_ref[:, cols[h]] = (acc / l_ref[h]).astype(o_ref.dtype)


def _sample_attn(q, cache_k, cache_v, kb, vb, c_cache, c_new, batch, t_new, past, chunk):
    assert past % chunk == 0
    nspec = pl.BlockSpec((t_new, D_MODEL), lambda b, s: (b, 0))
    cachespec = pl.BlockSpec((None, chunk * N_HEADS, HEAD_DIM), lambda b, s: (b, s, 0))
    ccspec = pl.BlockSpec((None, None, N_HEADS, chunk), lambda b, s: (b, s, 0, 0))
    cnspec = pl.BlockSpec((None, N_HEADS, LANES), lambda b, s: (b, 0, 0))
    head_rows = (batch, past * N_HEADS, HEAD_DIM)
    vmem = 2 * 2 * chunk * N_HEADS * HEAD_DIM * 4 + 8 * chunk * HEAD_DIM * 4 + (1 << 22)
    return pl.pallas_call(
        functools.partial(_sample_attn_kernel, chunk=chunk, t_new=t_new),
        grid=(batch, past // chunk),
        in_specs=[nspec, cachespec, cachespec, nspec, nspec, ccspec, cnspec],
        out_specs=nspec,
        out_shape=jax.ShapeDtypeStruct(q.shape, BF16),
        scratch_shapes=[pltpu.VMEM((N_HEADS, t_new, 1), F32), pltpu.VMEM((N_HEADS, t_new, 1), F32),
                        pltpu.VMEM((N_HEADS, t_new, HEAD_DIM), F32)],
        compiler_params=_params(("parallel", "arbitrary"), vmem),
        name="sample_attn",
    )(q, cache_k.reshape(head_rows), cache_v.reshape(head_rows), kb, vb, c_cache, c_new)


def _outproj_conv_kernel(x_ref, g_ref, o_ref, wa_ref, hist_ref, gn_ref, wb_ref, wc_ref, wu_ref, wg_ref, cw_ref,
                         wo_ref, y_ref, st_ref, tail_ref, *, tm):
    @pl.when(pl.program_id(1) == 0)
    def _():
        tail_ref[...] = jnp.zeros_like(tail_ref)
        tail_ref[8 - (CONV_K - 1):, :] = hist_ref[...]

    x, hb, cu = _conv_inputs(x_ref, g_ref, o_ref, wa_ref, gn_ref, wc_ref, wu_ref)
    tail = tail_ref[...]
    pos = lax.broadcasted_iota(jnp.int32, cu.shape, 0)
    y_ref[...] = _conv_outputs(x, hb, cu, pos, tail[6:7, :], tail[7:8, :], cw_ref, wg_ref, wb_ref, wo_ref)
    tail_ref[...] = cu[tm - 8:, :]
    st_ref[...] = cu[tm - (CONV_K - 1):, :]


def _conv_inputs(x_ref, g_ref, o_ref, wa_ref, gn_ref, wc_ref, wu_ref):
    gated = _silu(g_ref[...].astype(F32)) * o_ref[...].astype(F32)
    x = x_ref[...] + _dot(gated.astype(BF16), wa_ref[...])
    hb = (x * _rms_scale(x) * gn_ref[...]).astype(BF16)
    return x, hb, _dot(hb, wc_ref[...]) * _dot(hb, wu_ref[...])


def _conv_outputs(x, hb, cu, pos, h_m2, h_m1, cw_ref, wg_ref, wb_ref, wo_ref):
    s1 = jnp.where(pos == 0, h_m1, pltpu.roll(cu, 1, 0))
    s2 = jnp.where(pos == 0, h_m2, jnp.where(pos == 1, h_m1, pltpu.roll(cu, 2, 0)))
    cw = cw_ref[...]
    conv = cw[0:1, :] * s2 + cw[1:2, :] * s1 + cw[2:3, :] * cu
    mixed = _silu(_dot(hb, wg_ref[...])) * _dot(hb, wb_ref[...]) * conv
    return x + _dot(mixed.astype(BF16), wo_ref[...])


def _outproj_conv_short_kernel(x_ref, g_ref, o_ref, wa_ref, h0_ref, h1_ref, gn_ref, wb_ref, wc_ref, wu_ref, wg_ref,
                               cw_ref, wo_ref, y_ref, cu_ref, *, seq):
    x, hb, cu = _conv_inputs(x_ref, g_ref, o_ref, wa_ref, gn_ref, wc_ref, wu_ref)
    pos = lax.broadcasted_iota(jnp.int32, cu.shape, 0) & (seq - 1)
    y_ref[...] = _conv_outputs(x, hb, cu, pos, h0_ref[...], h1_ref[...], cw_ref, wg_ref, wb_ref, wo_ref)
    cu_ref[...] = cu


def _outproj_conv_short(x, gate, o, wa, hist, gn, wb, wc, wu, wg, cw, wo, batch, seq):
    n = batch * seq
    assert seq & (seq - 1) == 0 and seq >= CONV_K - 1 and n % 8 == 0
    h0 = jnp.repeat(hist[:, 0, :], seq, axis=0)
    h1 = jnp.repeat(hist[:, 1, :], seq, axis=0)
    args = (x, gate, o, wa, h0, h1, gn, wb, wc, wu, wg, cw, wo)
    vmem = 2 * 2 * 6 * D_MODEL * D_MODEL + 2 * n * D_MODEL * 24 + 10 * n * D_MODEL * 4
    y, cu = pl.pallas_call(
        functools.partial(_outproj_conv_short_kernel, seq=seq),
        out_shape=[jax.ShapeDtypeStruct(x.shape, F32), jax.ShapeDtypeStruct(x.shape, F32)],
        compiler_params=_params((), vmem),
        name="outproj_conv_short",
    )(*args)
    return y, cu.reshape(batch, seq, D_MODEL)[:, seq - (CONV_K - 1):, :]


def _outproj_conv(x, gate, o, wa, hist, gn, wb, wc, wu, wg, cw, wo, batch, seq, tm):
    assert seq % tm == 0 and tm % 8 == 0 and tm >= 8
    nt = seq // tm
    row = pl.BlockSpec((tm, D_MODEL), lambda b, t: (b * nt + t, 0))
    st = pl.BlockSpec((None, CONV_K - 1, D_MODEL), lambda b, t: (b, 0, 0))
    full = lambda a: pl.BlockSpec(a.shape, lambda b, t: (0, 0))
    w_bytes = 2 * 2 * 6 * D_MODEL * D_MODEL
    io_bytes = 2 * tm * D_MODEL * (4 + 2 + 2 + 4)
    tmp_bytes = 10 * tm * D_MODEL * 4
    return pl.pallas_call(
        functools.partial(_outproj_conv_kernel, tm=tm),
        grid=(batch, nt),
        in_specs=[row, row, row, full(wa), st, full(gn), full(wb), full(wc), full(wu), full(wg), full(cw),
                  full(wo)],
        out_specs=[row, st],
        out_shape=[jax.ShapeDtypeStruct(x.shape, F32),
                   jax.ShapeDtypeStruct((batch, CONV_K - 1, D_MODEL), F32)],
        scratch_shapes=[pltpu.VMEM((8, D_MODEL), F32)],
        compiler_params=_params(("parallel", "arbitrary"), w_bytes + io_bytes + tmp_bytes),
        name="outproj_conv",
    )(x, gate, o, wa, hist, gn, wb, wc, wu, wg, cw, wo)


def _row(v, width=None):
    v = v.astype(F32).reshape(1, -1)
    if width is not None and v.shape[1] < width:
        v = jnp.pad(v, ((0, 0), (0, width - v.shape[1])))
    return v


def _attn_weights(g_norm, w_in, b_f, g_q, g_k, w_out):
    aw = N_HEADS * HEAD_DIM
    wq, wk, wv, wg = (w_in[:, i * aw:(i + 1) * aw].astype(BF16) for i in range(4))
    wf = jnp.pad(w_in[:, 4 * aw:], ((0, 0), (0, LANES - N_HEADS))).astype(BF16)
    return (_row(g_norm), wq, wk, wv, wg, wf, _row(b_f, LANES), _row(g_q), _row(g_k)), w_out.astype(BF16)


def _heads_to_rows(lf, batch, length):
    return jnp.transpose(lf.reshape(batch, length, N_HEADS), (0, 2, 1)).reshape(batch * N_HEADS, length)


def _attn_prompt_layer(x, params, batch, seq, tm, tk):
    inw, wo = _attn_weights(*params)
    q, k, kb, v, vb, gate, lf = _attn_inproj(x, *inw, tm=tm)
    c = _cumsum_lanes(_heads_to_rows(lf, batch, seq))
    c = c.reshape(batch * N_HEADS, seq // tk, 1, tk)
    g_q, g_k = params[3], params[4]
    logit_bound = 1.01 * HEAD_DIM ** 0.5 * jnp.max(jnp.abs(g_q)) * jnp.max(jnp.abs(g_k))
    o = lax.cond(logit_bound < FAST_LOGIT_BOUND,
                 functools.partial(_flash_prompt, batch=batch, seq=seq, tk=tk, bounded=True),
                 functools.partial(_flash_prompt, batch=batch, seq=seq, tk=tk, bounded=False),
                 q, kb, vb, c)
    return (gate, o, wo), k, v, lf


def _attn_sample_layer(x, cache_k, cache_v, cache_logf, params, batch, t_new):
    past = cache_k.shape[1]
    inw, wo = _attn_weights(*params)
    q, k, kb, v, vb, gate, lf = _attn_inproj(x, *inw, tm=x.shape[0])
    lf_all = jnp.concatenate([cache_logf.astype(F32), lf.reshape(batch, t_new, N_HEADS)], axis=1)
    total = past + t_new
    padded = -(-total // LANES) * LANES
    lf_rows = _heads_to_rows(lf_all.reshape(batch * total, N_HEADS), batch, total)
    lf_rows = jnp.pad(lf_rows, ((0, 0), (0, padded - total)))
    c = _cumsum_lanes(lf_rows).reshape(batch, N_HEADS, padded)
    chunk = 2048
    c_cache = jnp.transpose(c[:, :, :past].reshape(batch, N_HEADS, past // chunk, chunk), (0, 2, 1, 3))
    c_new = c[:, :, past:past + LANES]
    o = _sample_attn(q, cache_k, cache_v, kb, vb, c_cache, c_new, batch, t_new, past, chunk)
    return (gate, o, wo), k, v, lf


def _conv_weights(g_norm, w_in, conv_w, w_out):
    w = D_MODEL
    wb, wc, wu, wg = (w_in[:, i * w:(i + 1) * w].astype(BF16) for i in range(4))
    return _row(g_norm), wb, wc, wu, wg, conv_w.astype(F32), w_out.astype(BF16)


def kernel(x_prompt, x_sample, cache_k_l0, cache_v_l0, cache_logf_l0, state_conv_l1, cache_k_l2, cache_v_l2, cache_logf_l2, state_conv_l3, norm_l0, w_in_l0, b_f_l0, qnorm_l0, knorm_l0, w_out_l0, norm_l1, w_in_l1, conv_w_l1, w_out_l1, norm_l2, w_in_l2, b_f_l2, qnorm_l2, knorm_l2, w_out_l2, norm_l3, w_in_l3, conv_w_l3, w_out_l3):
    batch, seq, d = x_prompt.shape
    dec_batch, dec_seq, _ = x_sample.shape
    caches = [(cache_k_l0, cache_v_l0, cache_logf_l0), (state_conv_l1,),
              (cache_k_l2, cache_v_l2, cache_logf_l2), (state_conv_l3,)]
    params = [(norm_l0, w_in_l0, b_f_l0, qnorm_l0, knorm_l0, w_out_l0),
              (norm_l1, w_in_l1, conv_w_l1, w_out_l1),
              (norm_l2, w_in_l2, b_f_l2, qnorm_l2, knorm_l2, w_out_l2),
              (norm_l3, w_in_l3, conv_w_l3, w_out_l3)]
    tm, tk = 512, 512
    yp = x_prompt.reshape(batch * seq, d)
    ys = x_sample.reshape(dec_batch * dec_seq, d)
    outs = []
    for i in range(0, len(params), 2):
        mix_p, kp, vp, lfp = _attn_prompt_layer(yp, params[i], batch, seq, tm, tk)
        mix_s, ks, vs, lfs = _attn_sample_layer(ys, *caches[i], params[i], dec_batch, dec_seq)
        outs += [kp.reshape(batch, seq, N_HEADS, HEAD_DIM), vp.reshape(batch, seq, N_HEADS, HEAD_DIM),
                 lfp.reshape(batch, seq, N_HEADS),
                 ks.reshape(dec_batch, dec_seq, N_HEADS, HEAD_DIM),
                 vs.reshape(dec_batch, dec_seq, N_HEADS, HEAD_DIM),
                 lfs.reshape(dec_batch, dec_seq, N_HEADS)]
        cw = _conv_weights(*params[i + 1])
        zero_hist = jnp.zeros((batch, CONV_K - 1, d), F32)
        yp, cp = _outproj_conv(yp, *mix_p, zero_hist, *cw, batch=batch, seq=seq, tm=tm)
        ys, cs = _outproj_conv_short(ys, *mix_s, caches[i + 1][0].astype(F32), *cw, batch=dec_batch, seq=dec_seq)
        outs += [cp, cs]
    return (yp.reshape(batch, seq, d), ys.reshape(dec_batch, dec_seq, d), *outs)
```
